```python
import math
import jax, jax.numpy as jnp
from jax import lax
import numpy as np

D_MODEL = 1024
BATCH = 1
SEQ = 16384
DEPTH = 2
DEC_BATCH = 32
DEC_SEQ = 4
PAST_LEN = 16384
PAGE_SIZE = 128

HEAD_DIM = 64
HA = 8
HB = 8
N_IDX_HEADS = 8
D_IDX = 64
TOPK_TOK = 256
MOBA_BLOCK = 256
MOBA_TOPK = 3
HC = 8
N_MEM = 256
HM = 4
HD_MEM = 128
ROPE_THETA = 10000.0
EPS = 1e-6
Q_BLK = 128
N_EVEN = (DEPTH + 1) // 2
N_ODD = DEPTH // 2
W_A = HA * HEAD_DIM
W_B = HB * HEAD_DIM
W_IQ = N_IDX_HEADS * D_IDX
W_C = HC * 2 * HEAD_DIM
W_M = HM * HD_MEM
EVEN_SPLITS = (W_A, W_A, W_A, W_A, W_B, W_B, W_B, W_B, W_IQ, D_IDX, N_IDX_HEADS)
ODD_SPLITS = (W_C, W_C, W_C, W_C)
EVEN_IN = sum(EVEN_SPLITS)

kernel_name = 'hybrid_dsa_moba_diffattn_step'


def rms_norm(x, g):
    xf = x.astype(jnp.float32)
    y = xf * lax.rsqrt(jnp.mean(xf * xf, axis=-1, keepdims=True) + EPS)
    return (y * g.astype(jnp.float32)).astype(x.dtype)


def rope(x, pos):
    d = x.shape[-1]
    inv = ROPE_THETA ** (-jnp.arange(0, d, 2, dtype=jnp.float32) / d)
    ang = pos.astype(jnp.float32)[:, None] * inv[None, :]
    shp = (1, pos.shape[0]) + (1,) * (x.ndim - 3) + (d // 2,)
    cos = jnp.cos(ang).reshape(shp)
    sin = jnp.sin(ang).reshape(shp)
    xf = x.astype(jnp.float32)
    x1, x2 = xf[..., :d // 2], xf[..., d // 2:]
    return jnp.concatenate([x1 * cos - x2 * sin, x2 * cos + x1 * sin], axis=-1).astype(x.dtype)


def split_cols(z, sizes):
    return jnp.split(z, np.cumsum(sizes)[:-1].tolist(), axis=-1)


def paged_take(pool, page_table, rows, heads=None):
    b = jnp.arange(rows.shape[0]).reshape((-1,) + (1,) * (rows.ndim - 1))
    phys = page_table[b, rows // PAGE_SIZE]
    off = rows % PAGE_SIZE
    if heads is None:
        return pool[phys, off]
    return pool[phys, off, heads]


def paged_all(pool, page_table):
    g = pool[page_table]
    return g.reshape((page_table.shape[0], -1) + pool.shape[2:])


def attend_parts(parts):
    s = jnp.concatenate([p[0] for p in parts], axis=-1)
    p = jax.nn.softmax(s, axis=-1)
    out = None
    start = 0
    for s_i, v_i, eq in parts:
        n = s_i.shape[-1]
        o = jnp.einsum(eq, p[..., start:start + n].astype(v_i.dtype), v_i)
        out = o if out is None else out + o
        start += n
    return out


def indexer_scores(qi, ki, wi):
    dots = jnp.einsum('bqhd,bsd->bqhs', qi.astype(jnp.float32), ki.astype(jnp.float32))
    return jnp.einsum('bqh,bqhs->bqs', wi.astype(jnp.float32), jax.nn.relu(dots))


def sparse_attend(q, kg, vg, ok):
    s = jnp.einsum('bqhd,bqnhd->bqhn', q, kg).astype(jnp.float32) * HEAD_DIM ** -0.5
    s = jnp.where(ok[:, :, None, :], s, -jnp.inf)
    return attend_parts([(s, vg, 'bqhn,bqnhd->bqhd')])


def even_proj(h, w_in, pos):
    B, T, _ = h.shape
    qa, ka, va, ga, qb, kb, vb, gb, qi, ki, wi = split_cols(h @ w_in, EVEN_SPLITS)
    qa = rope(qa.reshape(B, T, HA, HEAD_DIM), pos)
    ka = rope(ka.reshape(B, T, HA, HEAD_DIM), pos)
    va = va.reshape(B, T, HA, HEAD_DIM)
    qb = rope(qb.reshape(B, T, HB, HEAD_DIM), pos)
    kb = rope(kb.reshape(B, T, HB, HEAD_DIM), pos)
    vb = vb.reshape(B, T, HB, HEAD_DIM)
    qi = rope(qi.reshape(B, T, N_IDX_HEADS, D_IDX), pos)
    ki = rope(ki, pos)
    wi = wi * (N_IDX_HEADS * D_IDX) ** -0.5
    return qa, ka, va, ga, qb, kb, vb, gb, qi, ki, wi


def even_out(oa, ga, ob, gb, w_out):
    B, T = ga.shape[:2]
    z = jnp.concatenate([oa.reshape(B, T, W_A) * jax.nn.silu(ga),
                         ob.reshape(B, T, W_B) * jax.nn.silu(gb)], axis=-1)
    return z @ w_out


def dsa_prompt(q, k, v, qi, ki, wi):
    B, S, H, d = q.shape
    n_sel = min(TOPK_TOK, S // 4)
    b = jnp.arange(B)[:, None, None]
    keys_pos = jnp.arange(S)

    def block(i):
        t0 = i * Q_BLK
        sl = lambda a: lax.dynamic_slice_in_dim(a, t0, Q_BLK, axis=1)
        tq = t0 + jnp.arange(Q_BLK)
        score = indexer_scores(sl(qi), ki, sl(wi))
        score = jnp.where((keys_pos[None, :] <= tq[:, None])[None], score, -jnp.inf)
        _, idx = lax.top_k(score, n_sel)
        ok = idx <= tq[None, :, None]
        return sparse_attend(sl(q), k[b, idx], v[b, idx], ok)

    o = lax.map(block, jnp.arange(S // Q_BLK))
    return jnp.moveaxis(o, 0, 1).reshape(B, S, H, d)


def dsa_decode(q, k, v, qi, ki, wi, pool_k, pool_v, pool_ki, pt):
    B, T = q.shape[:2]
    L = PAST_LEN + T
    n_sel = min(TOPK_TOK, L // 4)
    ki_all = jnp.concatenate([paged_all(pool_ki, pt), ki], axis=1)
    tpos = PAST_LEN + jnp.arange(T)
    score = indexer_scores(qi, ki_all, wi)
    score = jnp.where((jnp.arange(L)[None, :] <= tpos[:, None])[None], score, -jnp.inf)
    _, idx = lax.top_k(score, n_sel)
    ok = idx <= tpos[None, :, None]
    b = jnp.arange(B)[:, None, None]
    rp = jnp.minimum(idx, PAST_LEN - 1)
    rn = jnp.clip(idx - PAST_LEN, 0, T - 1)
    is_new = (idx >= PAST_LEN)[..., None, None]
    kg = jnp.where(is_new, k[b, rn], paged_take(pool_k, pt, rp))
    vg = jnp.where(is_new, v[b, rn], paged_take(pool_v, pt, rp))
    return sparse_attend(q, kg, vg, ok)


def moba_prompt(q, k, v):
    B, S, H, d = q.shape
    scale = HEAD_DIM ** -0.5
    nb = -(-S // MOBA_BLOCK)
    pad = nb * MOBA_BLOCK - S
    kp = jnp.pad(k, ((0, 0), (0, pad), (0, 0), (0, 0)))
    vp = jnp.pad(v, ((0, 0), (0, pad), (0, 0), (0, 0)))
    kmean = kp.astype(jnp.float32).reshape(B, nb, MOBA_BLOCK, H, d).mean(2).astype(k.dtype)
    nsel = min(MOBA_TOPK, nb)
    bi = jnp.arange(B)[:, None, None, None, None]
    hi = jnp.arange(H)[None, None, :, None, None]
    blk_off = jnp.arange(MOBA_BLOCK)

    def block(i):
        t0 = i * Q_BLK
        qb = lax.dynamic_slice_in_dim(q, t0, Q_BLK, axis=1)
        tq = t0 + jnp.arange(Q_BLK)
        n_past = tq // MOBA_BLOCK
        gate = jnp.einsum('bqhd,bnhd->bqhn', qb, kmean).astype(jnp.float32)
        gate = jnp.where((jnp.arange(nb)[None, :] < n_past[:, None])[None, :, None, :], gate, -jnp.inf)
        _, sel = lax.top_k(gate, nsel)
        sel_ok = sel < n_past[None, :, None, None]
        rows = sel[..., None] * MOBA_BLOCK + blk_off
        kg = kp[bi, rows, hi]
        vg = vp[bi, rows, hi]
        s_sel = jnp.einsum('bqhd,bqhnkd->bqhnk', qb, kg).astype(jnp.float32) * scale
        s_sel = jnp.where(sel_ok[..., None], s_sel, -jnp.inf).reshape(B, Q_BLK, H, nsel * MOBA_BLOCK)
        bs = (t0 // MOBA_BLOCK) * MOBA_BLOCK
        ko = lax.dynamic_slice_in_dim(kp, bs, MOBA_BLOCK, axis=1)
        vo = lax.dynamic_slice_in_dim(vp, bs, MOBA_BLOCK, axis=1)
        s_own = jnp.einsum('bqhd,bkhd->bqhk', qb, ko).astype(jnp.float32) * scale
        s_own = jnp.where(((bs + blk_off)[None, :] <= tq[:, None])[None, :, None, :], s_own, -jnp.inf)
        return attend_parts([(s_sel, vg.reshape(B, Q_BLK, H, nsel * MOBA_BLOCK, d), 'bqhn,bqhnd->bqhd'),
                             (s_own, vo, 'bqhk,bkhd->bqhd')])

    o = lax.map(block, jnp.arange(S // Q_BLK))
    return jnp.moveaxis(o, 0, 1).reshape(B, S, H, d)


def moba_decode(q, k, v, pool_k, pool_v, pt):
    B, T, H, d = q.shape
    scale = HEAD_DIM ** -0.5
    n_full = PAST_LEN // MOBA_BLOCK
    bs = n_full * MOBA_BLOCK
    r = PAST_LEN - bs
    blk_off = jnp.arange(MOBA_BLOCK)
    parts = []
    if n_full > 0:
        kmean = paged_all(pool_k, pt)[:, :bs].astype(jnp.float32).reshape(
            B, n_full, MOBA_BLOCK, H, d).mean(2).astype(q.dtype)
        nsel = min(MOBA_TOPK, n_full)
        _, sel = lax.top_k(jnp.einsum('bthd,bnhd->bthn', q, kmean).astype(jnp.float32), nsel)
        rows = sel[..., None] * MOBA_BLOCK + blk_off
        hi = jnp.arange(H)[None, None, :, None, None]
        kg = paged_take(pool_k, pt, rows, hi)
        vg = paged_take(pool_v, pt, rows, hi)
        s_sel = (jnp.einsum('bthd,bthnkd->bthnk', q, kg).astype(jnp.float32) * scale).reshape(
            B, T, H, nsel * MOBA_BLOCK)
        parts.append((s_sel, vg.reshape(B, T, H, nsel * MOBA_BLOCK, d), 'bthn,bthnd->bthd'))
    if r > 0:
        rows = jnp.broadcast_to(bs + jnp.arange(r), (B, r))
        ko = paged_take(pool_k, pt, rows)
        vo = paged_take(pool_v, pt, rows)
        parts.append((jnp.einsum('bthd,brhd->bthr', q, ko).astype(jnp.float32) * scale, vo,
                      'bthr,brhd->bthd'))
    causal = jnp.arange(T)[None, :] <= jnp.arange(T)[:, None]
    s_new = jnp.einsum('bthd,bahd->btha', q, k).astype(jnp.float32) * scale
    s_new = jnp.where(causal[None, :, None, :], s_new, -jnp.inf)
    parts.append((s_new, v, 'btha,bahd->bthd'))
    return attend_parts(parts)


def odd_proj(h, w_in, pos):
    B, T, _ = h.shape
    q, k, v, g = split_cols(h @ w_in, ODD_SPLITS)
    q = rope(q.reshape(B, T, HC, 2, HEAD_DIM), pos)
    k = rope(k.reshape(B, T, HC, 2, HEAD_DIM), pos)
    return q, k, v.reshape(B, T, HC, 2 * HEAD_DIM), g


def diff_lambda(lq1, lk1, lq2, lk2, lam_init):
    f = lambda a, b: jnp.exp(jnp.sum(a.astype(jnp.float32) * b.astype(jnp.float32)))
    return f(lq1, lk1) - f(lq2, lk2) + lam_init


def diff_prompt(q, k, v, lam):
    B, S, H, _, d = q.shape
    keys_pos = jnp.arange(S)

    def block(i):
        t0 = i * Q_BLK
        qb = lax.dynamic_slice_in_dim(q, t0, Q_BLK, axis=1)
        tq = t0 + jnp.arange(Q_BLK)
        s = jnp.einsum('bqhmd,bshmd->bhmqs', qb, k).astype(jnp.float32) * d ** -0.5
        s = jnp.where(keys_pos[None, :] <= tq[:, None], s, -jnp.inf)
        p = jax.nn.softmax(s, axis=-1)
        a = p[:, :, 0] - lam * p[:, :, 1]
        return jnp.einsum('bhqs,bshe->bqhe', a.astype(v.dtype), v)

    o = lax.map(block, jnp.arange(S // Q_BLK))
    return jnp.moveaxis(o, 0, 1).reshape(B, S, H, 2 * d)


def diff_decode(q, k, v, pool_k, pool_v, pt, lam):
    B, T, H, _, d = q.shape
    scale = d ** -0.5

    def update(carry, s, vblk, eq):
        m, l, acc = carry
        m_new = jnp.maximum(m, s.max(-1))
        corr = jnp.exp(m - m_new)
        p = jnp.exp(s - m_new[..., None])
        acc = acc * corr[..., None] + jnp.einsum(eq, p.astype(vblk.dtype), vblk).astype(jnp.float32)
        return (m_new, l * corr + p.sum(-1), acc)

    def step(carry, pages):
        s = jnp.einsum('bthmd,bphmd->bhmtp', q, pool_k[pages]).astype(jnp.float32) * scale
        return update(carry, s, pool_v[pages], 'bhmtp,bphe->bhmte'), None

    init = (jnp.full((B, H, 2, T), -jnp.inf, jnp.float32),
            jnp.zeros((B, H, 2, T), jnp.float32),
            jnp.zeros((B, H, 2, T, 2 * d), jnp.float32))
    carry, _ = lax.scan(step, init, pt.T)
    causal = jnp.arange(T)[None, :] <= jnp.arange(T)[:, None]
    s_new = jnp.einsum('bthmd,bahmd->bhmta', q, k).astype(jnp.float32) * scale
    s_new = jnp.where(causal, s_new, -jnp.inf)
    _, l, acc = update(carry, s_new, v, 'bhmta,bahe->bhmte')
    o = acc / l[..., None]
    o = o[:, :, 0] - lam * o[:, :, 1]
    return jnp.transpose(o, (0, 2, 1, 3)).astype(v.dtype)


def diff_out(o, g, g_sub, lam_init, w_out):
    B, T = g.shape[:2]
    o = rms_norm(o, g_sub) * (1.0 - lam_init)
    return (o.reshape(B, T, W_C) * jax.nn.silu(g)) @ w_out


def mem_kv(mem, w_mk, w_mv):
    B, N, _ = mem.shape
    return (mem @ w_mk).reshape(B, N, HM, HD_MEM), (mem @ w_mv).reshape(B, N, HM, HD_MEM)


def mem_attend(h, mk, mv, w_mq, w_mo):
    B, T, _ = h.shape
    q, g = split_cols(h @ w_mq, (W_M, W_M))
    q = q.reshape(B, T, HM, HD_MEM)
    s = jnp.einsum('bthd,bnhd->bthn', q, mk).astype(jnp.float32) * HD_MEM ** -0.5
    p = jax.nn.softmax(s, axis=-1)
    o = jnp.einsum('bthn,bnhd->bthd', p.astype(mv.dtype), mv).reshape(B, T, W_M)
    return (o * jax.nn.silu(g)) @ w_mo


def setup_inputs(seed: int = 0) -> dict:
    key = jax.random.key(seed)
    ks = jax.random.split(key, 32)
    nrm = lambda i, shape, scale=1.0: jax.random.normal(ks[i], shape, jnp.float32) * scale
    n_pages = PAST_LEN // PAGE_SIZE
    n_used = DEC_BATCH * n_pages
    n_pool = n_used + max(1, n_used // 4)
    page_table = jax.random.permutation(ks[0], n_pool)[:n_used].reshape(DEC_BATCH, n_pages).astype(jnp.int32)
    D = D_MODEL
    return {
        'x_prompt': nrm(1, (BATCH, SEQ, D)),
        'x_sample': nrm(2, (DEC_BATCH, DEC_SEQ, D)),
        'cache_a_k': nrm(3, (N_EVEN, n_pool, PAGE_SIZE, HA, HEAD_DIM)),
        'cache_a_v': nrm(4, (N_EVEN, n_pool, PAGE_SIZE, HA, HEAD_DIM)),
        'cache_idx_k': nrm(5, (N_EVEN, n_pool, PAGE_SIZE, D_IDX)),
        'cache_b_k': nrm(6, (N_EVEN, n_pool, PAGE_SIZE, HB, HEAD_DIM)),
        'cache_b_v': nrm(7, (N_EVEN, n_pool, PAGE_SIZE, HB, HEAD_DIM)),
        'cache_c_k': nrm(8, (N_ODD, n_pool, PAGE_SIZE, HC, 2, HEAD_DIM)),
        'cache_c_v': nrm(9, (N_ODD, n_pool, PAGE_SIZE, HC, 2 * HEAD_DIM)),
        'cache_mem_k': nrm(10, (DEPTH, DEC_BATCH, N_MEM, HM, HD_MEM)),
        'cache_mem_v': nrm(11, (DEPTH, DEC_BATCH, N_MEM, HM, HD_MEM)),
        'page_table': page_table,
        'mem_prompt': nrm(12, (BATCH, N_MEM, D)),
        'g_mix': 1.0 + nrm(13, (DEPTH, D), 0.02),
        'g_mem': 1.0 + nrm(14, (DEPTH, D), 0.02),
        'g_final': 1.0 + nrm(15, (D,), 0.02),
        'w_in_even': nrm(16, (N_EVEN, D, EVEN_IN), D ** -0.5),
        'w_out_even': nrm(17, (N_EVEN, W_A + W_B, D), (W_A + W_B) ** -0.5),
        'w_in_odd': nrm(18, (N_ODD, D, 4 * W_C), D ** -0.5),
        'w_out_odd': nrm(19, (N_ODD, W_C, D), W_C ** -0.5),
        'lam_q1': nrm(20, (N_ODD, HEAD_DIM), 0.1),
        'lam_k1': nrm(21, (N_ODD, HEAD_DIM), 0.1),
        'lam_q2': nrm(22, (N_ODD, HEAD_DIM), 0.1),
        'lam_k2': nrm(23, (N_ODD, HEAD_DIM), 0.1),
        'g_subln': 1.0 + nrm(24, (N_ODD, 2 * HEAD_DIM), 0.02),
        'w_mq': nrm(25, (DEPTH, D, 2 * W_M), D ** -0.5),
        'w_mk': nrm(26, (DEPTH, D, W_M), D ** -0.5),
        'w_mv': nrm(27, (DEPTH, D, W_M), D ** -0.5),
        'w_mo': nrm(28, (DEPTH, W_M, D), W_M ** -0.5),
    }


def reference(x_prompt, x_sample, cache_a_k, cache_a_v, cache_idx_k, cache_b_k, cache_b_v,
              cache_c_k, cache_c_v, cache_mem_k, cache_mem_v, page_table, mem_prompt,
              g_mix, g_mem, g_final, w_in_even, w_out_even, w_in_odd, w_out_odd,
              lam_q1, lam_k1, lam_q2, lam_k2, g_subln, w_mq, w_mk, w_mv, w_mo):
    pos_p = jnp.arange(x_prompt.shape[1])
    pos_s = PAST_LEN + jnp.arange(x_sample.shape[1])
    xp, xs = x_prompt, x_sample
    pa_k, pa_v, pidx_k, pb_k, pb_v, pc_k, pc_v, pm_k, pm_v = [], [], [], [], [], [], [], [], []
    sa_k, sa_v, sidx_k, sb_k, sb_v, sc_k, sc_v = [], [], [], [], [], [], []
    for l in range(DEPTH):
        i = l // 2
        hp = rms_norm(xp, g_mix[l])
        hs = rms_norm(xs, g_mix[l])
        if l % 2 == 0:
            qa, ka, va, ga, qb, kb, vb, gb, qi, ki, wi = even_proj(hp, w_in_even[i], pos_p)
            oa = dsa_prompt(qa, ka, va, qi, ki, wi)
            ob = moba_prompt(qb, kb, vb)
            xp = xp + even_out(oa, ga, ob, gb, w_out_even[i])
            pa_k.append(ka); pa_v.append(va); pidx_k.append(ki); pb_k.append(kb); pb_v.append(vb)
            qa, ka, va, ga, qb, kb, vb, gb, qi, ki, wi = even_proj(hs, w_in_even[i], pos_s)
            oa = dsa_decode(qa, ka, va, qi, ki, wi, cache_a_k[i], cache_a_v[i], cache_idx_k[i], page_table)
            ob = moba_decode(qb, kb, vb, cache_b_k[i], cache_b_v[i], page_table)
            xs = xs + even_out(oa, ga, ob, gb, w_out_even[i])
            sa_k.append(ka); sa_v.append(va); sidx_k.append(ki); sb_k.append(kb); sb_v.append(vb)
        else:
            lam_init = 0.8 - 0.6 * math.exp(-0.3 * l)
            lam = diff_lambda(lam_q1[i], lam_k1[i], lam_q2[i], lam_k2[i], lam_init)
            q, k, v, g = odd_proj(hp, w_in_odd[i], pos_p)
            xp = xp + diff_out(diff_prompt(q, k, v, lam), g, g_subln[i], lam_init, w_out_odd[i])
            pc_k.append(k); pc_v.append(v)
            q, k, v, g = odd_proj(hs, w_in_odd[i], pos_s)
            o = diff_decode(q, k, v, cache_c_k[i], cache_c_v[i], page_table, lam)
            xs = xs + diff_out(o, g, g_subln[i], lam_init, w_out_odd[i])
            sc_k.append(k); sc_v.append(v)
        mk, mv = mem_kv(mem_prompt, w_mk[l], w_mv[l])
        xp = xp + mem_attend(rms_norm(xp, g_mem[l]), mk, mv, w_mq[l], w_mo[l])
        xs = xs + mem_attend(rms_norm(xs, g_mem[l]), cache_mem_k[l], cache_mem_v[l], w_mq[l], w_mo[l])
        pm_k.append(mk); pm_v.append(mv)
    y_prompt = rms_norm(xp, g_final)
    y_sample = rms_norm(xs, g_final)
    return (y_prompt, y_sample,
            jnp.stack(pa_k), jnp.stack(pa_v), jnp.stack(pidx_k), jnp.stack(pb_k), jnp.stack(pb_v),
            jnp.stack(pc_k), jnp.stack(pc_v), jnp.stack(pm_k), jnp.stack(pm_v),
            jnp.stack(sa_k), jnp.stack(sa_v), jnp.stack(sidx_k), jnp.stack(sb_k), jnp.stack(sb_v),
            jnp.stack(sc_k), jnp.stack(sc_v))
```

```python
import functools
import math

import jax
import jax.numpy as jnp
from jax import lax
from jax.experimental import pallas as pl
from jax.experimental.pallas import tpu as pltpu

F32 = jnp.float32
BF16 = jnp.bfloat16
I32 = jnp.int32

HEAD_DIM = 64
PAGE = 128
N_IDX = 8
TOPK_TOK = 256
MOBA_BLOCK = 256
MOBA_TOPK = 3
ROPE_THETA = 10000.0
EPS = 1e-6
LANES = 128
NEG = -1e30
INT_MIN = -2147483648
KEY_NEG_INF = -2139095041
VMEM_LIMIT = 56 * 1024 * 1024


def _cparams(sem):
    return pltpu.CompilerParams(dimension_semantics=sem, vmem_limit_bytes=VMEM_LIMIT)


def _dot_nt(a, b):
    return lax.dot_general(a, b, (((1,), (1,)), ((), ())), preferred_element_type=F32)


def _dot(a, b):
    return jnp.dot(a, b, preferred_element_type=F32)


def _tile_lanes(x, reps):
    return x if reps == 1 else jnp.concatenate([x] * reps, axis=1)


def _rope_tables(pos):
    inv = ROPE_THETA ** (-jnp.arange(0, HEAD_DIM, 2, dtype=F32) / HEAD_DIM)
    ang = pos.astype(F32)[:, None] * inv[None, :]
    cos, sin = jnp.cos(ang), jnp.sin(ang)
    cos128 = jnp.concatenate([cos] * 4, axis=1)
    sin128 = jnp.concatenate([-sin, sin, -sin, sin], axis=1)
    return cos128, sin128


def _rope128(z, c, s):
    lane = lax.broadcasted_iota(I32, z.shape, 1)
    first = (lane & 63) < 32
    partner = jnp.where(first, pltpu.roll(z, 96, axis=1), pltpu.roll(z, 32, axis=1))
    return z * c + partner * s


def _proj_kernel(*refs, groups, norm, has_rope):
    x_ref, g_ref, w_ref = refs[0], refs[1], refs[2]
    pos = 3
    if has_rope:
        c = refs[3][...]
        s = refs[4][...]
        pos = 5
    outs = refs[pos:]
    x = x_ref[...]
    if norm:
        x = x * lax.rsqrt(jnp.mean(x * x, axis=-1, keepdims=True) + EPS) * g_ref[...]
    xb = x.astype(BF16)
    oi = 0
    for start, width, kind, scale, dup in groups:
        z = _dot(xb, w_ref[:, start:start + width])
        if kind == "rope":
            z = jnp.concatenate(
                [_rope128(z[:, a:a + LANES], c, s) for a in range(0, width, LANES)], axis=1)
        elif kind == "tail":
            lane = lax.broadcasted_iota(I32, z.shape, 1)
            z = jnp.where(lane < 64, _rope128(z, c, s),
                          jnp.where(lane < 64 + N_IDX, z * scale, 0.0))
        outs[oi][...] = z
        oi += 1
        if dup:
            outs[oi][...] = z.astype(BF16)
            oi += 1


def _rms_proj(x, g, w, groups, cos=None, sin=None, norm=True):
    M, D = x.shape
    tm = min(256, M)
    has_rope = cos is not None
    in_specs = [pl.BlockSpec((tm, D), lambda i: (i, 0)),
                pl.BlockSpec((1, D), lambda i: (0, 0)),
                pl.BlockSpec(w.shape, lambda i: (0, 0))]
    args = [x, g.reshape(1, D), w]
    if has_rope:
        in_specs += [pl.BlockSpec((tm, LANES), lambda i: (i, 0))] * 2
        args += [cos, sin]
    out_shape, out_specs = [], []
    for _, width, _, _, dup in groups:
        out_shape.append(jax.ShapeDtypeStruct((M, width), F32))
        out_specs.append(pl.BlockSpec((tm, width), lambda i: (i, 0)))
        if dup:
            out_shape.append(jax.ShapeDtypeStruct((M, width), BF16))
            out_specs.append(pl.BlockSpec((tm, width), lambda i: (i, 0)))
    return pl.pallas_call(
        functools.partial(_proj_kernel, groups=tuple(groups), norm=norm, has_rope=has_rope),
        grid=(M // tm,), in_specs=in_specs, out_specs=out_specs, out_shape=out_shape,
        compiler_params=_cparams(("parallel",)), name="rms_proj")(*args)


def _silu(g):
    return g * (1.0 / (1.0 + jnp.exp(-g)))


def _gated_out_kernel(*refs, widths, subnorm, post_scale, final_norm):
    resid_ref, w_ref = refs[0], refs[1]
    pos = 2
    if subnorm:
        gsub = refs[pos][...]
        pos += 1
    if final_norm:
        gfin = refs[pos][...]
        pos += 1
    acc = resid_ref[...]
    off = 0
    for wd in widths:
        o = refs[pos][...]
        g = refs[pos + 1][...]
        pos += 2
        if subnorm:
            parts = []
            for a in range(0, wd, LANES):
                oh = o[:, a:a + LANES]
                parts.append(oh * lax.rsqrt(jnp.mean(oh * oh, axis=-1, keepdims=True) + EPS)
                             * gsub * post_scale)
            o = jnp.concatenate(parts, axis=1)
        z = (o * _silu(g)).astype(BF16)
        acc = acc + _dot(z, w_ref[off:off + wd, :])
        off += wd
    out_ref = refs[pos]
    out_ref[...] = acc
    if final_norm:
        refs[pos + 1][...] = acc * lax.rsqrt(jnp.mean(acc * acc, axis=-1, keepdims=True) + EPS) * gfin


def _gated_out(resid, w, parts, gsub=None, post_scale=1.0, gfin=None):
    M, D = resid.shape
    tm = min(256, M)
    widths = tuple(o.shape[1] for o, _ in parts)
    in_specs = [pl.BlockSpec((tm, D), lambda i: (i, 0)), pl.BlockSpec(w.shape, lambda i: (0, 0))]
    args = [resid, w]
    if gsub is not None:
        in_specs.append(pl.BlockSpec((1, LANES), lambda i: (0, 0)))
        args.append(gsub.reshape(1, LANES))
    if gfin is not None:
        in_specs.append(pl.BlockSpec((1, D), lambda i: (0, 0)))
        args.append(gfin.reshape(1, D))
    for o, g in parts:
        wd = o.shape[1]
        in_specs += [pl.BlockSpec((tm, wd), lambda i: (i, 0))] * 2
        args += [o, g]
    out_shape = [jax.ShapeDtypeStruct((M, D), F32)]
    out_specs = [pl.BlockSpec((tm, D), lambda i: (i, 0))]
    if gfin is not None:
        out_shape.append(jax.ShapeDtypeStruct((M, D), F32))
        out_specs.append(pl.BlockSpec((tm, D), lambda i: (i, 0)))
    res = pl.pallas_call(
        functools.partial(_gated_out_kernel, widths=widths, subnorm=gsub is not None,
                          post_scale=post_scale, final_norm=gfin is not None),
        grid=(M // tm,), in_specs=in_specs, out_specs=out_specs, out_shape=out_shape,
        compiler_params=_cparams(("parallel",)), name="gated_out")(*args)
    return res if gfin is not None else res[0]


def _mem_attn_kernel(q_ref, k_ref, v_ref, o_ref, *, heads, hd):
    scale = hd ** -0.5
    outs = []
    for h in range(heads):
        q = q_ref[0, :, h * hd:(h + 1) * hd].astype(BF16)
        k = k_ref[0, :, h * hd:(h + 1) * hd].astype(BF16)
        v = v_ref[0, :, h * hd:(h + 1) * hd].astype(BF16)
        s = _dot_nt(q, k) * scale
        m = jnp.max(s, axis=1, keepdims=True)
        p = jnp.exp(s - m)
        l = jnp.sum(p, axis=1, keepdims=True)
        outs.append(_dot((p / l).astype(BF16), v))
    o_ref[0] = jnp.concatenate(outs, axis=1)


def _mem_attn(q3, mk3, mv3, heads, hd):
    B, T, W = q3.shape
    Bk, N, _ = mk3.shape
    tq = min(256, T)
    kidx = (lambda b, i: (b, 0, 0)) if Bk == B else (lambda b, i: (0, 0, 0))
    return pl.pallas_call(
        functools.partial(_mem_attn_kernel, heads=heads, hd=hd),
        grid=(B, T // tq),
        in_specs=[pl.BlockSpec((1, tq, W), lambda b, i: (b, i, 0)),
                  pl.BlockSpec((1, N, W), kidx), pl.BlockSpec((1, N, W), kidx)],
        out_specs=pl.BlockSpec((1, tq, W), lambda b, i: (b, i, 0)),
        out_shape=jax.ShapeDtypeStruct((B, T, W), F32),
        compiler_params=_cparams(("parallel", "parallel")), name="mem_attn")(q3, mk3, mv3)


def _float_key(x):
    bits = lax.bitcast_convert_type(x, I32)
    return jnp.where(bits < 0, bits ^ 0x7FFFFFFF, bits)


def _topk_bias(key_ref, n, out_ref, lim, *, kk, R, kc):
    slabs = kc // LANES

    def count(thr_b, strict):
        def body(c, acc):
            kt = key_ref[c]
            for a in range(slabs):
                blk = kt[:, a * LANES:(a + 1) * LANES]
                hit = (blk > thr_b) if strict else (blk >= thr_b)
                acc = acc + jnp.where(hit, 1, 0)
            return acc
        acc = lax.fori_loop(0, n, body, jnp.zeros((R, LANES), I32))
        return jnp.sum(acc, axis=1, keepdims=True)

    def search(b, t_u):
        cand_u = t_u | jnp.left_shift(jnp.int32(1), 31 - b)
        cnt = count(cand_u ^ INT_MIN, False)
        return jnp.where(cnt >= kk, cand_u, t_u)

    t_u = lax.fori_loop(0, 32, search, jnp.zeros((R, LANES), I32))
    thr = t_u ^ INT_MIN
    cnt_gt = count(thr, True)
    need = kk - cnt_gt
    n_eq = count(thr, False) - cnt_gt
    surplus = jnp.where(n_eq > need, jnp.where(thr[:, :1] > KEY_NEG_INF, 1, 0), 0)
    has_ties = jnp.max(surplus)
    thr_full = _tile_lanes(thr, slabs)
    lane = lax.broadcasted_iota(I32, (R, kc), 1)

    @pl.when(has_ties == 0)
    def _():
        def body(c, carry):
            sel = jnp.where(key_ref[c] >= thr_full, 0.0, NEG)
            out_ref[c] = jnp.where(c * kc + lane <= lim, sel, NEG).astype(out_ref.dtype)
            return carry
        lax.fori_loop(0, n, body, 0)

    @pl.when(has_ties != 0)
    def _():
        r_i = lax.broadcasted_iota(I32, (kc, kc), 0)
        c_i = lax.broadcasted_iota(I32, (kc, kc), 1)
        upper = jnp.where(r_i <= c_i, 1.0, 0.0).astype(BF16)
        need_f = need.astype(F32)

        def body(c, run):
            kt = key_ref[c]
            eq = jnp.where(kt == thr_full, 1.0, 0.0)
            rank = run + _dot(eq.astype(BF16), upper)
            take_eq = jnp.where(rank <= need_f, eq, 0.0)
            sel = jnp.where(kt > thr_full, 0.0, jnp.where(take_eq > 0.5, 0.0, NEG))
            out_ref[c] = jnp.where(c * kc + lane <= lim, sel, NEG).astype(out_ref.dtype)
            return run + jnp.sum(eq, axis=1, keepdims=True)
        lax.fori_loop(0, n, body, jnp.zeros((R, 1), F32))


def _dsa_select_kernel(qi_ref, wq_ref, kt_ref, out_ref, key_sc, qh_sc, wb_sc, *, tq, kc, nc, kk):
    i = pl.program_id(0)
    t0 = i * tq
    n = (t0 + tq + kc - 1) // kc
    lane = lax.broadcasted_iota(I32, (tq, LANES), 1)
    q = qi_ref[...]
    wq = wq_ref[...]
    for h in range(N_IDX):
        slab = q[:, (h // 2) * LANES:(h // 2 + 1) * LANES]
        if h % 2 == 1:
            slab = pltpu.roll(slab, 64, axis=1)
        qh_sc[h] = jnp.where(lane < 64, slab, 0.0).astype(BF16)
        wcol = jnp.sum(jnp.where(lane == 64 + h, wq, 0.0), axis=1, keepdims=True)
        wb_sc[h] = jnp.broadcast_to(wcol, (tq, LANES))
    row_t = t0 + lax.broadcasted_iota(I32, (tq, 1), 0)
    lane_k = lax.broadcasted_iota(I32, (tq, kc), 1)
    slabs = kc // LANES

    def chunk(c, carry):
        kt = kt_ref[pl.ds(pl.multiple_of(c * kc, kc), kc), :]
        acc = jnp.zeros((tq, kc), F32)
        for h in range(N_IDX):
            d = _dot_nt(qh_sc[h], kt)
            acc = acc + _tile_lanes(wb_sc[h], slabs) * jnp.maximum(d, 0.0)
        sc = jnp.where(c * kc + lane_k <= row_t, acc, -jnp.inf)
        key_sc[c] = _float_key(sc)
        return carry
    lax.fori_loop(0, n, chunk, 0)

    _topk_bias(key_sc, n, out_ref, row_t, kk=kk, R=tq, kc=kc)

    def fill(c, carry):
        out_ref[c] = jnp.full((tq, kc), NEG, out_ref.dtype)
        return carry
    lax.fori_loop(n, nc, fill, 0)


def _dsa_select(qi, tail, tail_bf, kk, tq=128, kc=512):
    S = qi.shape[0]
    kc = min(kc, S)
    nc = S // kc
    return pl.pallas_call(
        functools.partial(_dsa_select_kernel, tq=tq, kc=kc, nc=nc, kk=kk),
        grid=(S // tq,),
        in_specs=[pl.BlockSpec((tq, qi.shape[1]), lambda i: (i, 0)),
                  pl.BlockSpec((tq, LANES), lambda i: (i, 0)),
                  pl.BlockSpec((S, LANES), lambda i: (0, 0))],
        out_specs=pl.BlockSpec((nc, tq, kc), lambda i: (0, i, 0)),
        out_shape=jax.ShapeDtypeStruct((nc, S, kc), BF16),
        scratch_shapes=[pltpu.VMEM((nc, tq, kc), I32),
                        pltpu.VMEM((N_IDX, tq, LANES), BF16),
                        pltpu.VMEM((N_IDX, tq, LANES), F32)],
        compiler_params=_cparams(("parallel",)), name="dsa_select")(qi, tail, tail_bf)


def _select_rows_kernel(sc_ref, lim_ref, out_ref, key_sc, *, R, kc, nc, kk):
    lim = lim_ref[...]
    lane = lax.broadcasted_iota(I32, (R, kc), 1)

    def prep(c, carry):
        key_sc[c] = _float_key(jnp.where(c * kc + lane <= lim, sc_ref[c], -jnp.inf))
        return carry
    lax.fori_loop(0, nc, prep, 0)
    _topk_bias(key_sc, nc, out_ref, lim, kk=kk, R=R, kc=kc)


def _select_rows(scores_cm, lim, kk):
    nc, R, kc = scores_cm.shape
    return pl.pallas_call(
        functools.partial(_select_rows_kernel, R=R, kc=kc, nc=nc, kk=kk),
        grid=(1,),
        in_specs=[pl.BlockSpec((nc, R, kc), lambda i: (0, 0, 0)),
                  pl.BlockSpec((R, 1), lambda i: (0, 0))],
        out_specs=pl.BlockSpec((nc, R, kc), lambda i: (0, 0, 0)),
        out_shape=jax.ShapeDtypeStruct((nc, R, kc), BF16),
        scratch_shapes=[pltpu.VMEM((nc, R, kc), I32)],
        compiler_params=_cparams(("arbitrary",)), name="select_rows")(scores_cm, lim)


def _online_softmax(s, m_prev, l_prev, reps):
    m_new = jnp.maximum(m_prev, jnp.max(s, axis=1, keepdims=True))
    alpha = jnp.exp(m_prev - m_new)
    p = jnp.exp(s - _tile_lanes(m_new, reps))
    l_new = alpha * l_prev + jnp.sum(p, axis=1, keepdims=True)
    return p, alpha, m_new, l_new


def _moba_top_blocks(gate, n_past):
    lane = lax.broadcasted_iota(I32, gate.shape, 1)
    g = jnp.where(lane < n_past, gate, -jnp.inf)
    ind = jnp.zeros(gate.shape, F32)
    for _ in range(MOBA_TOPK):
        mx = jnp.max(g, axis=1, keepdims=True)
        am = jnp.min(jnp.where(g == mx, lane, 2 * LANES), axis=1, keepdims=True)
        hit = jnp.where(mx > -jnp.inf, jnp.where(lane == am, 1.0, 0.0), 0.0)
        ind = ind + hit
        g = jnp.where(lane == am, -jnp.inf, g)
    return ind


def _block_expand(j, kc):
    r_i = lax.broadcasted_iota(I32, (LANES, kc), 0)
    c_i = lax.broadcasted_iota(I32, (LANES, kc), 1)
    return jnp.where(r_i == (j * kc + c_i) // MOBA_BLOCK, 1.0, 0.0).astype(BF16)


def _flash_kernel(*refs, mode, tq, kc, nj, npairs):
    i = pl.program_id(0)
    j = pl.program_id(1)
    q_ref, k_ref, v_ref = refs[0], refs[1], refs[2]
    pos = 3
    if mode == "bias":
        b_ref = refs[pos]; pos += 1
    elif mode == "moba":
        km_ref = refs[pos]; pos += 1
    elif mode == "diff":
        lam_ref = refs[pos]; pos += 1
    o_ref = refs[pos]; pos += 1
    m_sc, l_sc, acc_sc = refs[pos], refs[pos + 1], refs[pos + 2]
    if mode == "moba":
        sel_sc = refs[pos + 3]
    jmax = ((i + 1) * tq - 1) // kc
    reps = kc // LANES
    lane = lax.broadcasted_iota(I32, (tq, LANES), 1)
    lo = lane < 64

    def q_maps(p):
        q2 = q_ref[:, p * LANES:(p + 1) * LANES] * (HEAD_DIM ** -0.5)
        return (jnp.where(lo, q2, 0.0).astype(BF16), jnp.where(lo, 0.0, q2).astype(BF16))

    @pl.when(j == 0)
    def _():
        m_sc[...] = jnp.full(m_sc.shape, NEG, F32)
        l_sc[...] = jnp.zeros(l_sc.shape, F32)
        acc_sc[...] = jnp.zeros(acc_sc.shape, F32)
        if mode == "moba":
            n_past = (i * tq) // MOBA_BLOCK
            for p in range(npairs):
                km2 = km_ref[:, p * LANES:(p + 1) * LANES].astype(BF16)
                for e, qm in enumerate(q_maps(p)):
                    sel_sc[2 * p + e] = _moba_top_blocks(_dot_nt(qm, km2), n_past).astype(BF16)

    @pl.when(j <= jmax)
    def _():
        row_t = i * tq + lax.broadcasted_iota(I32, (tq, kc), 0)
        key_s = j * kc + lax.broadcasted_iota(I32, (tq, kc), 1)
        if mode == "bias":
            bias = b_ref[0].astype(F32)
        elif mode == "moba":
            expand = _block_expand(j, kc)
            own = (key_s // MOBA_BLOCK) == (row_t // MOBA_BLOCK)
            causal = key_s <= row_t
        else:
            causal = key_s <= row_t
        for p in range(npairs):
            k2 = k_ref[:, p * LANES:(p + 1) * LANES]
            v2 = v_ref[:, p * LANES:(p + 1) * LANES]
            pv, al = [], []
            for e, qm in enumerate(q_maps(p)):
                h = 2 * p + e
                s = _dot_nt(qm, k2)
                if mode == "bias":
                    s = s + bias
                elif mode == "moba":
                    picked = _dot(sel_sc[h], expand)
                    s = jnp.where(own, jnp.where(causal, s, NEG), jnp.where(picked > 0.5, s, NEG))
                else:
                    s = jnp.where(causal, s, NEG)
                pr, alpha, m_new, l_new = _online_softmax(s, m_sc[h], l_sc[h], reps)
                m_sc[h] = m_new
                l_sc[h] = l_new
                pv.append(_dot(pr.astype(BF16), v2))
                al.append(alpha)
            if mode == "diff":
                acc_sc[2 * p] = acc_sc[2 * p] * al[0] + pv[0]
                acc_sc[2 * p + 1] = acc_sc[2 * p + 1] * al[1] + pv[1]
            else:
                acc_sc[p] = acc_sc[p] * jnp.where(lo, al[0], al[1]) + jnp.where(lo, pv[0], pv[1])

    @pl.when(j == nj - 1)
    def _():
        for p in range(npairs):
            if mode == "diff":
                o = acc_sc[2 * p] / l_sc[2 * p] - lam_ref[0, 0] * (acc_sc[2 * p + 1] / l_sc[2 * p + 1])
            else:
                o = acc_sc[p] / jnp.where(lo, l_sc[2 * p], l_sc[2 * p + 1])
            o_ref[:, p * LANES:(p + 1) * LANES] = o


def _flash(q, k_bf, v_bf, mode, bias=None, kmean=None, lam=None, tq=256, kc=512):
    S, W = q.shape
    tq, kc = min(tq, S), min(kc, S)
    ni, nj = S // tq, S // kc
    npairs = W // LANES
    jm = lambda i, j: jnp.minimum(j, ((i + 1) * tq - 1) // kc)
    in_specs = [pl.BlockSpec((tq, W), lambda i, j: (i, 0)),
                pl.BlockSpec((kc, W), lambda i, j: (jm(i, j), 0)),
                pl.BlockSpec((kc, W), lambda i, j: (jm(i, j), 0))]
    args = [q, k_bf, v_bf]
    nmaps = 2 * npairs
    scratch = [pltpu.VMEM((nmaps, tq, LANES), F32), pltpu.VMEM((nmaps, tq, LANES), F32),
               pltpu.VMEM((nmaps if mode == "diff" else npairs, tq, LANES), F32)]
    if mode == "bias":
        in_specs.append(pl.BlockSpec((1, tq, kc), lambda i, j: (jm(i, j), i, 0)))
        args.append(bias)
    elif mode == "moba":
        in_specs.append(pl.BlockSpec(kmean.shape, lambda i, j: (0, 0)))
        args.append(kmean)
        scratch.append(pltpu.VMEM((nmaps, tq, LANES), BF16))
    elif mode == "diff":
        in_specs.append(pl.BlockSpec(memory_space=pltpu.SMEM))
        args.append(lam.reshape(1, 1))
    return pl.pallas_call(
        functools.partial(_flash_kernel, mode=mode, tq=tq, kc=kc, nj=nj, npairs=npairs),
        grid=(ni, nj), in_specs=in_specs,
        out_specs=pl.BlockSpec((tq, W), lambda i, j: (i, 0)),
        out_shape=jax.ShapeDtypeStruct((S, W), F32), scratch_shapes=scratch,
        compiler_params=_cparams(("parallel", "arbitrary")), name="flash_" + mode)(*args)


def _block_mean_kernel(k_ref, o_ref, *, nb, per):
    i = pl.program_id(0)

    @pl.when(i < nb // per)
    def _():
        x = k_ref[...]
        o_ref[...] = jnp.mean(x.reshape(per, MOBA_BLOCK, x.shape[1]), axis=1)

    @pl.when(i >= nb // per)
    def _():
        o_ref[...] = jnp.zeros(o_ref.shape, F32)


def _block_means(kb):
    S, W = kb.shape
    nb, per = S // MOBA_BLOCK, 8
    last = nb // per - 1
    return pl.pallas_call(
        functools.partial(_block_mean_kernel, nb=nb, per=per),
        grid=(LANES // per,),
        in_specs=[pl.BlockSpec((per * MOBA_BLOCK, W), lambda i: (jnp.minimum(i, last), 0))],
        out_specs=pl.BlockSpec((per, W), lambda i: (i, 0)),
        out_shape=jax.ShapeDtypeStruct((LANES, W), F32),
        compiler_params=_cparams(("arbitrary",)), name="block_means")(kb)


def _page_specs(shape, ppc, nj):
    def make(p):
        return pl.BlockSpec((1,) + tuple(shape),
                            lambda b, j, pt: (pt[b, jnp.minimum(j, nj - 1) * ppc + p], 0, 0))
    return [make(p) for p in range(ppc)]


def _pool_t(cache):
    NP = cache.shape[0]
    nd = cache.ndim
    return jnp.transpose(cache, (0,) + tuple(range(2, nd)) + (1,)).reshape(NP, -1, PAGE)


def _idx_scores_kernel(pt_ref, q_ref, w_ref, new_ref, *refs, ppc, nj, T):
    j = pl.program_id(1)
    pages, o_ref = refs[:ppc], refs[ppc]
    kc = ppc * PAGE
    q = q_ref[0].astype(BF16)
    w = w_ref[0]

    def scores(kt):
        d = jnp.maximum(_dot(q, kt), 0.0) * _tile_lanes(w, kt.shape[1] // LANES)
        return jnp.sum(d.reshape(T, N_IDX, kt.shape[1]), axis=1)

    @pl.when(j < nj)
    def _():
        kt = jnp.concatenate([r[0] for r in pages], axis=1).astype(BF16)
        o_ref[0, 0] = scores(kt)

    @pl.when(j == nj)
    def _():
        sc = scores(new_ref[0].astype(BF16))
        o_ref[0, 0] = jnp.concatenate([sc, jnp.zeros((T, kc - PAGE), F32)], axis=1)


def _idx_scores(pt, q_rows, w_rows, new_page, pool, ppc=4):
    B = q_rows.shape[0]
    T = q_rows.shape[1] // N_IDX
    nj = pt.shape[1] // ppc
    kc = ppc * PAGE
    grid_spec = pltpu.PrefetchScalarGridSpec(
        num_scalar_prefetch=1, grid=(B, nj + 1),
        in_specs=[pl.BlockSpec((1, T * N_IDX, HEAD_DIM), lambda b, j, pt: (b, 0, 0)),
                  pl.BlockSpec((1, T * N_IDX, LANES), lambda b, j, pt: (b, 0, 0)),
                  pl.BlockSpec((1, HEAD_DIM, PAGE), lambda b, j, pt: (b, 0, 0))]
        + _page_specs(pool.shape[1:], ppc, nj),
        out_specs=pl.BlockSpec((1, 1, T, kc), lambda b, j, pt: (j, b, 0, 0)))
    return pl.pallas_call(
        functools.partial(_idx_scores_kernel, ppc=ppc, nj=nj, T=T),
        grid_spec=grid_spec, out_shape=jax.ShapeDtypeStruct((nj + 1, B, T, kc), F32),
        compiler_params=_cparams(("parallel", "arbitrary")), name="idx_scores",
    )(pt, q_rows, w_rows, new_page, *([pool] * ppc))


def _pool_block_mean_kernel(pt_ref, *refs, npg):
    j = pl.program_id(1)
    pages, o_ref = refs[:npg], refs[npg]
    ppb = MOBA_BLOCK // PAGE
    lane = lax.broadcasted_iota(I32, o_ref.shape[1:], 1)

    @pl.when(j == 0)
    def _():
        o_ref[...] = jnp.zeros(o_ref.shape, F32)

    acc = o_ref[0]
    for blk in range(npg // ppb):
        tot = pages[blk * ppb][0]
        for a in range(1, ppb):
            tot = tot + pages[blk * ppb + a][0]
        mean = jnp.sum(tot, axis=1, keepdims=True) * (1.0 / MOBA_BLOCK)
        acc = jnp.where(lane == j * (npg // ppb) + blk, mean, acc)
    o_ref[0] = acc


def _pool_block_means(pt, pool_t, bpc=8):
    B, n_pages = pt.shape
    W = pool_t.shape[1]
    ppb = MOBA_BLOCK // PAGE
    nb = n_pages // ppb
    assert nb <= LANES
    bpc = min(bpc, nb)
    npg = bpc * ppb
    nj = nb // bpc
    grid_spec = pltpu.PrefetchScalarGridSpec(
        num_scalar_prefetch=1, grid=(B, nj),
        in_specs=_page_specs((W, PAGE), npg, nj),
        out_specs=pl.BlockSpec((1, W, LANES), lambda b, j, pt: (b, 0, 0)))
    return pl.pallas_call(
        functools.partial(_pool_block_mean_kernel, npg=npg),
        grid_spec=grid_spec, out_shape=jax.ShapeDtypeStruct((B, W, LANES), F32),
        compiler_params=_cparams(("parallel", "arbitrary")), name="pool_block_means",
    )(pt, *([pool_t] * npg))


def _paged_attn_kernel(pt_ref, *refs, mode, ppc, nj, T, G, W, Wv, vt):
    j = pl.program_id(1)
    q_ref, knew_ref, vnew_ref = refs[0], refs[1], refs[2]
    pos = 3
    if mode == "bias":
        b_ref = refs[pos]; pos += 1
    elif mode == "moba":
        km_ref = refs[pos]; pos += 1
    elif mode == "diff":
        lam_ref = refs[pos]; pos += 1
    kpages = refs[pos:pos + ppc]; pos += ppc
    vpages = refs[pos:pos + ppc]; pos += ppc
    o_ref = refs[pos]; pos += 1
    m_sc, l_sc, acc_sc = refs[pos], refs[pos + 1], refs[pos + 2]
    if mode == "moba":
        sel_sc = refs[pos + 3]
    R = T * G
    kc = ppc * PAGE
    row = lax.broadcasted_iota(I32, (R, W), 0)
    col = lax.broadcasted_iota(I32, (R, W), 1)
    q = q_ref[0] * (HEAD_DIM ** -0.5)
    qrep = jnp.concatenate([jnp.broadcast_to(q[t:t + 1], (G, W)) for t in range(T)], axis=0)
    qbd = jnp.where(col // HEAD_DIM == row % G, qrep, 0.0).astype(BF16)

    @pl.when(j == 0)
    def _():
        m_sc[...] = jnp.full(m_sc.shape, NEG, F32)
        l_sc[...] = jnp.zeros(l_sc.shape, F32)
        acc_sc[...] = jnp.zeros(acc_sc.shape, F32)
        if mode == "moba":
            n_full = (nj * kc) // MOBA_BLOCK
            sel_sc[...] = _moba_top_blocks(_dot(qbd, km_ref[0].astype(BF16)), n_full).astype(BF16)

    def update(s, v):
        pr, alpha, m_new, l_new = _online_softmax(s, m_sc[...], l_sc[...], s.shape[1] // LANES)
        m_sc[...] = m_new
        l_sc[...] = l_new
        pv = _dot_nt(pr.astype(BF16), v) if vt else _dot(pr.astype(BF16), v)
        acc_sc[...] = acc_sc[...] * _tile_lanes(alpha, Wv // LANES) + pv

    @pl.when(j < nj)
    def _():
        k = jnp.concatenate([r[0] for r in kpages], axis=1).astype(BF16)
        v = jnp.concatenate([r[0] for r in vpages], axis=1 if vt else 0).astype(BF16)
        s = _dot(qbd, k)
        if mode == "bias":
            b = b_ref[0, 0].astype(F32)
            s = s + jnp.concatenate([jnp.broadcast_to(b[t:t + 1], (G, kc)) for t in range(T)], axis=0)
        elif mode == "moba":
            s = jnp.where(_dot(sel_sc[...], _block_expand(j, kc)) > 0.5, s, NEG)
        update(s, v)

    @pl.when(j == nj)
    def _():
        s = _dot(qbd, knew_ref[0].astype(BF16))
        if mode == "bias":
            b = b_ref[0, 0][:, :PAGE].astype(F32)
            s = s + jnp.concatenate([jnp.broadcast_to(b[t:t + 1], (G, PAGE)) for t in range(T)], axis=0)
        else:
            r_i = lax.broadcasted_iota(I32, (R, PAGE), 0)
            c_i = lax.broadcasted_iota(I32, (R, PAGE), 1)
            s = jnp.where(c_i <= r_i // G, s, NEG)
        update(s, vnew_ref[0].astype(BF16))
        o = acc_sc[...] / _tile_lanes(l_sc[...], Wv // LANES)
        rv = lax.broadcasted_iota(I32, (R, Wv), 0)
        cv = lax.broadcasted_iota(I32, (R, Wv), 1)
        if mode == "diff":
            keep = cv // (2 * HEAD_DIM) == (rv % G) // 2
            coef = jnp.where((rv % G) % 2 == 0, 1.0, -lam_ref[0, 0])
            o = jnp.where(keep, o * coef, 0.0)
        else:
            o = jnp.where(cv // HEAD_DIM == rv % G, o, 0.0)
        o_ref[0] = jnp.sum(o.reshape(T, G, Wv), axis=1)


def _paged_attn(pt, q3, knew, vnew, pool_k, pool_v, mode, vt, bias=None, kmean=None, lam=None, ppc=4):
    B, T, W = q3.shape
    Wv = pool_v.shape[1] if vt else pool_v.shape[2]
    G = W // HEAD_DIM
    nj = pt.shape[1] // ppc
    kc = ppc * PAGE
    cmap = lambda b, j, pt: (b, 0, 0)
    in_specs = [pl.BlockSpec((1, T, W), cmap), pl.BlockSpec((1,) + knew.shape[1:], cmap),
                pl.BlockSpec((1,) + vnew.shape[1:], cmap)]
    args = [q3, knew, vnew]
    scratch = [pltpu.VMEM((T * G, LANES), F32), pltpu.VMEM((T * G, LANES), F32),
               pltpu.VMEM((T * G, Wv), F32)]
    if mode == "bias":
        in_specs.append(pl.BlockSpec((1, 1, T, kc), lambda b, j, pt: (j, b, 0, 0)))
        args.append(bias)
    elif mode == "moba":
        in_specs.append(pl.BlockSpec((1, W, LANES), cmap))
        args.append(kmean)
        scratch.append(pltpu.VMEM((T * G, LANES), BF16))
    elif mode == "diff":
        in_specs.append(pl.BlockSpec(memory_space=pltpu.SMEM))
        args.append(lam.reshape(1, 1))
    in_specs += _page_specs(pool_k.shape[1:], ppc, nj) + _page_specs(pool_v.shape[1:], ppc, nj)
    args += [pool_k] * ppc + [pool_v] * ppc
    grid_spec = pltpu.PrefetchScalarGridSpec(
        num_scalar_prefetch=1, grid=(B, nj + 1), in_specs=in_specs,
        out_specs=pl.BlockSpec((1, T, Wv), cmap), scratch_shapes=scratch)
    return pl.pallas_call(
        functools.partial(_paged_attn_kernel, mode=mode, ppc=ppc, nj=nj, T=T, G=G, W=W, Wv=Wv, vt=vt),
        grid_spec=grid_spec, out_shape=jax.ShapeDtypeStruct((B, T, Wv), F32),
        compiler_params=_cparams(("parallel", "arbitrary")), name="paged_" + mode)(pt, *args)


def _lambda_kernel(a_ref, o_ref, *, lam_init):
    a = a_ref[...]
    d1 = jnp.sum(a[0:1] * a[1:2], axis=1, keepdims=True)
    d2 = jnp.sum(a[2:3] * a[3:4], axis=1, keepdims=True)
    o_ref[...] = jnp.exp(d1) - jnp.exp(d2) + lam_init


def _diff_lambda(lq1, lk1, lq2, lk2, lam_init):
    a = jnp.zeros((8, LANES), F32).at[:4, :HEAD_DIM].set(jnp.stack([lq1, lk1, lq2, lk2]))
    return pl.pallas_call(functools.partial(_lambda_kernel, lam_init=lam_init),
                          out_shape=jax.ShapeDtypeStruct((1, 1), F32), name="diff_lambda")(a)


def _pad_new_page(x3):
    B, T, W = x3.shape
    return jnp.pad(x3, ((0, 0), (0, PAGE - T), (0, 0)))


def _new_page_t(x3):
    B, T, W = x3.shape
    return jnp.pad(jnp.swapaxes(x3, 1, 2), ((0, 0), (0, 0), (0, PAGE - T)))


def kernel(x_prompt, x_sample, cache_a_k, cache_a_v, cache_idx_k, cache_b_k, cache_b_v, cache_c_k, cache_c_v, cache_mem_k, cache_mem_v, page_table, mem_prompt, g_mix, g_mem, g_final, w_in_even, w_out_even, w_in_odd, w_out_odd, lam_q1, lam_k1, lam_q2, lam_k2, g_subln, w_mq, w_mk, w_mv, w_mo):
    Bp, S, D = x_prompt.shape
    Bs, T, _ = x_sample.shape
    assert Bp == 1
    depth = g_mix.shape[0]
    n_pages = page_table.shape[1]
    past = n_pages * PAGE
    assert past % MOBA_BLOCK == 0 and S % MOBA_BLOCK == 0
    NP = cache_a_k.shape[1]
    HA = cache_a_k.shape[3]
    HB = cache_b_k.shape[3]
    HC = cache_c_k.shape[3]
    HM, HDM = cache_mem_k.shape[3], cache_mem_k.shape[4]
    NM = cache_mem_k.shape[2]
    WA, WB, WC, WM = HA * HEAD_DIM, HB * HEAD_DIM, HC * 2 * HEAD_DIM, HM * HDM
    WIQ = N_IDX * HEAD_DIM
    Ms = Bs * T

    cos_p, sin_p = _rope_tables(jnp.arange(S))
    cos_s, sin_s = _rope_tables(jnp.tile(past + jnp.arange(T), Bs))
    xp = x_prompt.reshape(S, D)
    xs = x_sample.reshape(Ms, D)
    mem2 = mem_prompt.reshape(NM, D)
    pt = page_table.astype(I32)

    n_tail = 4 * WA + 4 * WB + WIQ
    even_groups = []
    for gi in range(8):
        kind = "rope" if gi in (0, 1, 4, 5) else "plain"
        even_groups.append((gi * WA, WA, kind, 1.0, gi in (1, 2, 5, 6)))
    even_groups.append((8 * WA, WIQ, "rope", 1.0, False))
    even_groups.append((n_tail, LANES, "tail", float(WIQ) ** -0.5, True))
    odd_groups = [(0, WC, "rope", 1.0, False), (WC, WC, "rope", 1.0, True),
                  (2 * WC, WC, "plain", 1.0, True), (3 * WC, WC, "plain", 1.0, False)]
    mem_groups = [(0, WM, "plain", 1.0, False), (WM, WM, "plain", 1.0, False)]

    outs = {}
    ones = jnp.ones((D,), F32)
    for l in range(depth):
        i = l // 2
        if l % 2 == 0:
            w_in = w_in_even[i]
            w_in = jnp.pad(w_in, ((0, 0), (0, n_tail + LANES - w_in.shape[1]))).astype(BF16)
            w_out = w_out_even[i].astype(BF16)
            (qa, ka, ka_b, va, va_b, ga, qb, kb, kb_b, vb, vb_b, gb, qi, tail, tail_b) = _rms_proj(
                xp, g_mix[l], w_in, even_groups, cos_p, sin_p)
            kk = min(TOPK_TOK, S // 4)
            bias = _dsa_select(qi, tail, tail_b, kk)
            oa = _flash(qa, ka_b, va_b, "bias", bias=bias)
            ob = _flash(qb, kb_b, vb_b, "moba", kmean=_block_means(kb))
            xp = _gated_out(xp, w_out, [(oa, ga), (ob, gb)])
            outs.setdefault("pa_k", []).append(ka.reshape(1, S, HA, HEAD_DIM))
            outs.setdefault("pa_v", []).append(va.reshape(1, S, HA, HEAD_DIM))
            outs.setdefault("pidx_k", []).append(tail[:, :HEAD_DIM].reshape(1, S, HEAD_DIM))
            outs.setdefault("pb_k", []).append(kb.reshape(1, S, HB, HEAD_DIM))
            outs.setdefault("pb_v", []).append(vb.reshape(1, S, HB, HEAD_DIM))
            (qa, ka, _, va, _, ga, qb, kb, _, vb, _, gb, qi, tail, _) = _rms_proj(
                xs, g_mix[l], w_in, even_groups, cos_s, sin_s)
            q_rows = qi.reshape(Bs, T * N_IDX, HEAD_DIM)
            w_rows = jnp.broadcast_to(tail[:, HEAD_DIM:HEAD_DIM + N_IDX].reshape(Bs, T * N_IDX, 1),
                                      (Bs, T * N_IDX, LANES))
            tail3 = tail.reshape(Bs, T, LANES)
            ki_new = _new_page_t(tail3[:, :, :HEAD_DIM])
            sc = _idx_scores(pt, q_rows, w_rows, ki_new, _pool_t(cache_idx_k[i]))
            ncs, _, _, kcs = sc.shape
            lim = (past + jnp.tile(jnp.arange(T, dtype=I32), Bs)).reshape(Ms, 1)
            bias_s = _select_rows(sc.reshape(ncs, Ms, kcs), lim, min(TOPK_TOK, (past + T) // 4))
            oa = _paged_attn(pt, qa.reshape(Bs, T, WA), _new_page_t(ka.reshape(Bs, T, WA)),
                             _new_page_t(va.reshape(Bs, T, WA)),
                             _pool_t(cache_a_k[i]), _pool_t(cache_a_v[i]),
                             "bias", True, bias=bias_s.reshape(ncs, Bs, T, kcs))
            pool_bk = _pool_t(cache_b_k[i])
            ob = _paged_attn(pt, qb.reshape(Bs, T, WB), _new_page_t(kb.reshape(Bs, T, WB)),
                             _new_page_t(vb.reshape(Bs, T, WB)),
                             pool_bk, _pool_t(cache_b_v[i]),
                             "moba", True, kmean=_pool_block_means(pt, pool_bk))
            xs = _gated_out(xs, w_out, [(oa.reshape(Ms, WA), ga), (ob.reshape(Ms, WB), gb)])
            outs.setdefault("sa_k", []).append(ka.reshape(Bs, T, HA, HEAD_DIM))
            outs.setdefault("sa_v", []).append(va.reshape(Bs, T, HA, HEAD_DIM))
            outs.setdefault("sidx_k", []).append(tail[:, :HEAD_DIM].reshape(Bs, T, HEAD_DIM))
            outs.setdefault("sb_k", []).append(kb.reshape(Bs, T, HB, HEAD_DIM))
            outs.setdefault("sb_v", []).append(vb.reshape(Bs, T, HB, HEAD_DIM))
        else:
            lam_init = 0.8 - 0.6 * math.exp(-0.3 * l)
            lam = _diff_lambda(lam_q1[i], lam_k1[i], lam_q2[i], lam_k2[i], lam_init)
            w_in = w_in_odd[i].astype(BF16)
            w_out = w_out_odd[i].astype(BF16)
            q, k, k_b, v, v_b, g = _rms_proj(xp, g_mix[l], w_in, odd_groups, cos_p, sin_p)
            o = _flash(q, k_b, v_b, "diff", lam=lam)
            xp = _gated_out(xp, w_out, [(o, g)], gsub=g_subln[i], post_scale=1.0 - lam_init)
            outs.setdefault("pc_k", []).append(k.reshape(1, S, HC, 2, HEAD_DIM))
            outs.setdefault("pc_v", []).append(v.reshape(1, S, HC, 2 * HEAD_DIM))
            q, k, _, v, _, g = _rms_proj(xs, g_mix[l], w_in, odd_groups, cos_s, sin_s)
            o = _paged_attn(pt, q.reshape(Bs, T, WC), _new_page_t(k.reshape(Bs, T, WC)),
                            _pad_new_page(v.reshape(Bs, T, WC)),
                            _pool_t(cache_c_k[i]), cache_c_v[i].reshape(NP, PAGE, WC),
                            "diff", False, lam=lam)
            xs = _gated_out(xs, w_out, [(o.reshape(Ms, WC), g)], gsub=g_subln[i], post_scale=1.0 - lam_init)
            outs.setdefault("sc_k", []).append(k.reshape(Bs, T, HC, 2, HEAD_DIM))
            outs.setdefault("sc_v", []).append(v.reshape(Bs, T, HC, 2 * HEAD_DIM))
        w_kv = jnp.concatenate([w_mk[l], w_mv[l]], axis=1).astype(BF16)
        mk, mv = _rms_proj(mem2, ones, w_kv, mem_groups, norm=False)
        w_q = w_mq[l].astype(BF16)
        w_o = w_mo[l].astype(BF16)
        last = l == depth - 1
        q, g = _rms_proj(xp, g_mem[l], w_q, mem_groups)
        o = _mem_attn(q.reshape(1, S, WM), mk.reshape(1, NM, WM), mv.reshape(1, NM, WM), HM, HDM)
        res = _gated_out(xp, w_o, [(o.reshape(S, WM), g)], gfin=g_final if last else None)
        xp, yp = res if last else (res, None)
        q, g = _rms_proj(xs, g_mem[l], w_q, mem_groups)
        o = _mem_attn(q.reshape(Bs, T, WM), cache_mem_k[l].reshape(Bs, NM, WM),
                      cache_mem_v[l].reshape(Bs, NM, WM), HM, HDM)
        res = _gated_out(xs, w_o, [(o.reshape(Ms, WM), g)], gfin=g_final if last else None)
        xs, ys = res if last else (res, None)
        outs.setdefault("pm_k", []).append(mk.reshape(1, NM, HM, HDM))
        outs.setdefault("pm_v", []).append(mv.reshape(1, NM, HM, HDM))

    st = lambda name: jnp.stack(outs[name])
    return (yp.reshape(1, S, D), ys.reshape(Bs, T, D),
            st("pa_k"), st("pa_v"), st("pidx_k"), st("pb_k"), st("pb_v"), st("pc_k"), st("pc_v"),
            st("pm_k"), st("pm_v"),
            st("sa_k"), st("sa_v"), st("sidx_k"), st("sb_k"), st("sb_v"), st("sc_k"), st("sc_v"))
```

```python
import functools
import math

import jax
import jax.numpy as jnp
from jax import lax
from jax.experimental import pallas as pl
from jax.experimental.pallas import tpu as pltpu

F32 = jnp.float32
BF16 = jnp.bfloat16
I32 = jnp.int32

HEAD_DIM = 64
PAGE = 128
N_IDX = 8
TOPK_TOK = 256
MOBA_BLOCK = 256
MOBA_TOPK = 3
ROPE_THETA = 10000.0
EPS = 1e-6
LANES = 128
NEG = -1e30
Q_SCALE = HEAD_DIM ** -0.5 * math.log2(math.e)
INT_MIN = -2147483648
KEY_NEG_INF = -2139095041
VMEM_LIMIT = 56 * 1024 * 1024
VMEM_LIMIT_LARGE = 60 * 1024 * 1024


def _cparams(sem, vmem=VMEM_LIMIT):
    return pltpu.CompilerParams(dimension_semantics=sem, vmem_limit_bytes=vmem)


def _dot_nt(a, b):
    return lax.dot_general(a, b, (((1,), (1,)), ((), ())), preferred_element_type=F32)


def _dot(a, b):
    return jnp.dot(a, b, preferred_element_type=F32)


def _tile_lanes(x, reps):
    return x if reps == 1 else jnp.concatenate([x] * reps, axis=1)


def _rope_tables(pos):
    inv = ROPE_THETA ** (-jnp.arange(0, HEAD_DIM, 2, dtype=F32) / HEAD_DIM)
    ang = pos.astype(F32)[:, None] * inv[None, :]
    cos, sin = jnp.cos(ang), jnp.sin(ang)
    cos128 = jnp.concatenate([cos] * 4, axis=1)
    sin128 = jnp.concatenate([-sin, sin, -sin, sin], axis=1)
    return cos128, sin128


def _rope128(z, c, s):
    lane = lax.broadcasted_iota(I32, z.shape, 1)
    first = (lane & 63) < 32
    partner = jnp.where(first, pltpu.roll(z, 96, axis=1), pltpu.roll(z, 32, axis=1))
    return z * c + partner * s


def _proj_kernel(*refs, groups, norm, has_rope):
    x_ref, g_ref, w_ref = refs[0], refs[1], refs[2]
    pos = 3
    if has_rope:
        c = refs[3][...]
        s = refs[4][...]
        pos = 5
    outs = refs[pos:]
    x = x_ref[...]
    if norm:
        x = x * lax.rsqrt(jnp.mean(x * x, axis=-1, keepdims=True) + EPS) * g_ref[...]
    xb = x.astype(BF16)
    oi = 0
    for start, width, kind, scale, dup in groups:
        z = _dot(xb, w_ref[:, start:start + width])
        if kind == "rope":
            z = jnp.concatenate(
                [_rope128(z[:, a:a + LANES], c, s) for a in range(0, width, LANES)], axis=1)
        elif kind == "tail":
            lane = lax.broadcasted_iota(I32, z.shape, 1)
            z = jnp.where(lane < 64, _rope128(z, c, s),
                          jnp.where(lane < 64 + N_IDX, z * scale, 0.0))
        outs[oi][...] = z
        oi += 1
        if dup:
            outs[oi][...] = z.astype(BF16)
            oi += 1


def _rms_proj(x, g, w, groups, cos=None, sin=None, norm=True):
    M, D = x.shape
    tm = min(256, M)
    has_rope = cos is not None
    in_specs = [pl.BlockSpec((tm, D), lambda i: (i, 0)),
                pl.BlockSpec((1, D), lambda i: (0, 0)),
                pl.BlockSpec(w.shape, lambda i: (0, 0))]
    args = [x, g.reshape(1, D), w]
    if has_rope:
        in_specs += [pl.BlockSpec((tm, LANES), lambda i: (i, 0))] * 2
        args += [cos, sin]
    out_shape, out_specs = [], []
    for _, width, _, _, dup in groups:
        out_shape.append(jax.ShapeDtypeStruct((M, width), F32))
        out_specs.append(pl.BlockSpec((tm, width), lambda i: (i, 0)))
        if dup:
            out_shape.append(jax.ShapeDtypeStruct((M, width), BF16))
            out_specs.append(pl.BlockSpec((tm, width), lambda i: (i, 0)))
    return pl.pallas_call(
        functools.partial(_proj_kernel, groups=tuple(groups), norm=norm, has_rope=has_rope),
        grid=(M // tm,), in_specs=in_specs, out_specs=out_specs, out_shape=out_shape,
        compiler_params=_cparams(("parallel",)), name="rms_proj")(*args)


def _silu(g):
    return g * (1.0 / (1.0 + jnp.exp(-g)))


def _gated_out_kernel(*refs, widths, subnorm, post_scale, final_norm):
    resid_ref, w_ref = refs[0], refs[1]
    pos = 2
    if subnorm:
        gsub = refs[pos][...]
        pos += 1
    if final_norm:
        gfin = refs[pos][...]
        pos += 1
    acc = resid_ref[...]
    off = 0
    for wd in widths:
        o = refs[pos][...]
        g = refs[pos + 1][...]
        pos += 2
        if subnorm:
            parts = []
            for a in range(0, wd, LANES):
                oh = o[:, a:a + LANES]
                parts.append(oh * lax.rsqrt(jnp.mean(oh * oh, axis=-1, keepdims=True) + EPS)
                             * gsub * post_scale)
            o = jnp.concatenate(parts, axis=1)
        z = (o * _silu(g)).astype(BF16)
        acc = acc + _dot(z, w_ref[off:off + wd, :])
        off += wd
    out_ref = refs[pos]
    out_ref[...] = acc
    if final_norm:
        refs[pos + 1][...] = acc * lax.rsqrt(jnp.mean(acc * acc, axis=-1, keepdims=True) + EPS) * gfin


def _gated_out(resid, w, parts, gsub=None, post_scale=1.0, gfin=None):
    M, D = resid.shape
    tm = min(256, M)
    widths = tuple(o.shape[1] for o, _ in parts)
    in_specs = [pl.BlockSpec((tm, D), lambda i: (i, 0)), pl.BlockSpec(w.shape, lambda i: (0, 0))]
    args = [resid, w]
    if gsub is not None:
        in_specs.append(pl.BlockSpec((1, LANES), lambda i: (0, 0)))
        args.append(gsub.reshape(1, LANES))
    if gfin is not None:
        in_specs.append(pl.BlockSpec((1, D), lambda i: (0, 0)))
        args.append(gfin.reshape(1, D))
    for o, g in parts:
        wd = o.shape[1]
        in_specs += [pl.BlockSpec((tm, wd), lambda i: (i, 0))] * 2
        args += [o, g]
    out_shape = [jax.ShapeDtypeStruct((M, D), F32)]
    out_specs = [pl.BlockSpec((tm, D), lambda i: (i, 0))]
    if gfin is not None:
        out_shape.append(jax.ShapeDtypeStruct((M, D), F32))
        out_specs.append(pl.BlockSpec((tm, D), lambda i: (i, 0)))
    res = pl.pallas_call(
        functools.partial(_gated_out_kernel, widths=widths, subnorm=gsub is not None,
                          post_scale=post_scale, final_norm=gfin is not None),
        grid=(M // tm,), in_specs=in_specs, out_specs=out_specs, out_shape=out_shape,
        compiler_params=_cparams(("parallel",)), name="gated_out")(*args)
    return res if gfin is not None else res[0]


def _mem_attn_kernel(q_ref, k_ref, v_ref, o_ref, *, heads, hd):
    scale = hd ** -0.5
    outs = []
    for h in range(heads):
        q = q_ref[0, :, h * hd:(h + 1) * hd].astype(BF16)
        k = k_ref[0, :, h * hd:(h + 1) * hd].astype(BF16)
        v = v_ref[0, :, h * hd:(h + 1) * hd].astype(BF16)
        s = _dot_nt(q, k) * scale
        m = jnp.max(s, axis=1, keepdims=True)
        p = jnp.exp(s - m)
        l = jnp.sum(p, axis=1, keepdims=True)
        outs.append(_dot((p / l).astype(BF16), v))
    o_ref[0] = jnp.concatenate(outs, axis=1)


def _mem_attn(q3, mk3, mv3, heads, hd):
    B, T, W = q3.shape
    Bk, N, _ = mk3.shape
    tq = min(256, T)
    kidx = (lambda b, i: (b, 0, 0)) if Bk == B else (lambda b, i: (0, 0, 0))
    return pl.pallas_call(
        functools.partial(_mem_attn_kernel, heads=heads, hd=hd),
        grid=(B, T // tq),
        in_specs=[pl.BlockSpec((1, tq, W), lambda b, i: (b, i, 0)),
                  pl.BlockSpec((1, N, W), kidx), pl.BlockSpec((1, N, W), kidx)],
        out_specs=pl.BlockSpec((1, tq, W), lambda b, i: (b, i, 0)),
        out_shape=jax.ShapeDtypeStruct((B, T, W), F32),
        compiler_params=_cparams(("parallel", "parallel")), name="mem_attn")(q3, mk3, mv3)


def _float_key(x):
    bits = lax.bitcast_convert_type(x, I32)
    return jnp.where(bits < 0, bits ^ 0x7FFFFFFF, bits)


def _topk_bias(key_ref, n, out_ref, lim, *, kk, R, kc):
    slabs = kc // LANES

    def count(thr_b, strict):
        def body(c, acc):
            kt = key_ref[c]
            for a in range(slabs):
                blk = kt[:, a * LANES:(a + 1) * LANES]
                hit = (blk > thr_b) if strict else (blk >= thr_b)
                acc = acc + jnp.where(hit, 1, 0)
            return acc
        acc = lax.fori_loop(0, n, body, jnp.zeros((R, LANES), I32))
        return jnp.sum(acc, axis=1, keepdims=True)

    def search(b, t_u):
        cand_u = t_u | jnp.left_shift(jnp.int32(1), 31 - b)
        cnt = count(cand_u ^ INT_MIN, False)
        return jnp.where(cnt >= kk, cand_u, t_u)

    t_u = lax.fori_loop(0, 32, search, jnp.zeros((R, LANES), I32))
    thr = t_u ^ INT_MIN
    cnt_gt = count(thr, True)
    need = kk - cnt_gt
    n_eq = count(thr, False) - cnt_gt
    surplus = jnp.where(n_eq > need, jnp.where(thr[:, :1] > KEY_NEG_INF, 1, 0), 0)
    has_ties = jnp.max(surplus)
    thr_full = _tile_lanes(thr, slabs)
    lane = lax.broadcasted_iota(I32, (R, kc), 1)

    @pl.when(has_ties == 0)
    def _():
        def body(c, carry):
            sel = jnp.where(key_ref[c] >= thr_full, 0.0, NEG)
            out_ref[c] = jnp.where(c * kc + lane <= lim, sel, NEG).astype(out_ref.dtype)
            return carry
        lax.fori_loop(0, n, body, 0)

    @pl.when(has_ties != 0)
    def _():
        r_i = lax.broadcasted_iota(I32, (kc, kc), 0)
        c_i = lax.broadcasted_iota(I32, (kc, kc), 1)
        upper = jnp.where(r_i <= c_i, 1.0, 0.0).astype(BF16)
        need_f = need.astype(F32)

        def body(c, run):
            kt = key_ref[c]
            eq = jnp.where(kt == thr_full, 1.0, 0.0)
            rank = run + _dot(eq.astype(BF16), upper)
            take_eq = jnp.where(rank <= need_f, eq, 0.0)
            sel = jnp.where(kt > thr_full, 0.0, jnp.where(take_eq > 0.5, 0.0, NEG))
            out_ref[c] = jnp.where(c * kc + lane <= lim, sel, NEG).astype(out_ref.dtype)
            return run + jnp.sum(eq, axis=1, keepdims=True)
        lax.fori_loop(0, n, body, jnp.zeros((R, 1), F32))


def _dsa_select_kernel(qi_ref, wq_ref, kt_ref, out_ref, key_sc, qh_sc, wb_sc, *, tq, kc, nc, kk):
    i = pl.program_id(0)
    t0 = i * tq
    n = (t0 + tq + kc - 1) // kc
    lane = lax.broadcasted_iota(I32, (tq, LANES), 1)
    q = qi_ref[...]
    wq = wq_ref[...]
    for h in range(N_IDX):
        slab = q[:, (h // 2) * LANES:(h // 2 + 1) * LANES]
        if h % 2 == 1:
            slab = pltpu.roll(slab, 64, axis=1)
        qh_sc[h] = jnp.where(lane < 64, slab, 0.0).astype(BF16)
        wcol = jnp.sum(jnp.where(lane == 64 + h, wq, 0.0), axis=1, keepdims=True)
        wb_sc[h] = jnp.broadcast_to(wcol, (tq, LANES))
    row_t = t0 + lax.broadcasted_iota(I32, (tq, 1), 0)
    lane_k = lax.broadcasted_iota(I32, (tq, kc), 1)
    slabs = kc // LANES

    def chunk(c, carry):
        kt = kt_ref[pl.ds(pl.multiple_of(c * kc, kc), kc), :]
        acc = jnp.zeros((tq, kc), F32)
        for h in range(N_IDX):
            d = _dot_nt(qh_sc[h], kt)
            acc = acc + _tile_lanes(wb_sc[h], slabs) * jnp.maximum(d, 0.0)
        sc = jnp.where(c * kc + lane_k <= row_t, acc, -jnp.inf)
        key_sc[c] = _float_key(sc)
        return carry
    lax.fori_loop(0, n, chunk, 0)

    _topk_bias(key_sc, n, out_ref, row_t, kk=kk, R=tq, kc=kc)

    def fill(c, carry):
        out_ref[c] = jnp.full((tq, kc), NEG, out_ref.dtype)
        return carry
    lax.fori_loop(n, nc, fill, 0)


def _dsa_select(qi, tail, tail_bf, kk, tq=128, kc=512):
    S = qi.shape[0]
    kc = min(kc, S)
    nc = S // kc
    return pl.pallas_call(
        functools.partial(_dsa_select_kernel, tq=tq, kc=kc, nc=nc, kk=kk),
        grid=(S // tq,),
        in_specs=[pl.BlockSpec((tq, qi.shape[1]), lambda i: (i, 0)),
                  pl.BlockSpec((tq, LANES), lambda i: (i, 0)),
                  pl.BlockSpec((S, LANES), lambda i: (0, 0))],
        out_specs=pl.BlockSpec((nc, tq, kc), lambda i: (0, i, 0)),
        out_shape=jax.ShapeDtypeStruct((nc, S, kc), BF16),
        scratch_shapes=[pltpu.VMEM((nc, tq, kc), I32),
                        pltpu.VMEM((N_IDX, tq, LANES), BF16),
                        pltpu.VMEM((N_IDX, tq, LANES), F32)],
        compiler_params=_cparams(("parallel",)), name="dsa_select")(qi, tail, tail_bf)


def _select_rows_kernel(sc_ref, lim_ref, out_ref, key_sc, *, R, kc, nc, kk):
    lim = lim_ref[...]
    lane = lax.broadcasted_iota(I32, (R, kc), 1)

    def prep(c, carry):
        key_sc[c] = _float_key(jnp.where(c * kc + lane <= lim, sc_ref[c], -jnp.inf))
        return carry
    lax.fori_loop(0, nc, prep, 0)
    _topk_bias(key_sc, nc, out_ref, lim, kk=kk, R=R, kc=kc)


def _select_rows(scores_cm, lim, kk):
    nc, R, kc = scores_cm.shape
    return pl.pallas_call(
        functools.partial(_select_rows_kernel, R=R, kc=kc, nc=nc, kk=kk),
        grid=(1,),
        in_specs=[pl.BlockSpec((nc, R, kc), lambda i: (0, 0, 0)),
                  pl.BlockSpec((R, 1), lambda i: (0, 0))],
        out_specs=pl.BlockSpec((nc, R, kc), lambda i: (0, 0, 0)),
        out_shape=jax.ShapeDtypeStruct((nc, R, kc), BF16),
        scratch_shapes=[pltpu.VMEM((nc, R, kc), I32)],
        compiler_params=_cparams(("arbitrary",)), name="select_rows")(scores_cm, lim)


def _online_softmax(s, m_prev, l_prev, reps):
    m_new = jnp.maximum(m_prev, jnp.max(s, axis=1, keepdims=True))
    alpha = jnp.exp2(m_prev - m_new)
    p = jnp.exp2(s - _tile_lanes(m_new, reps))
    l_new = alpha * l_prev + jnp.sum(p, axis=1, keepdims=True)
    return p, alpha, m_new, l_new


def _moba_top_blocks(gate, n_past):
    lane = lax.broadcasted_iota(I32, gate.shape, 1)
    g = jnp.where(lane < n_past, gate, -jnp.inf)
    ind = jnp.zeros(gate.shape, F32)
    for _ in range(MOBA_TOPK):
        mx = jnp.max(g, axis=1, keepdims=True)
        am = jnp.min(jnp.where(g == mx, lane, 2 * LANES), axis=1, keepdims=True)
        hit = jnp.where(mx > -jnp.inf, jnp.where(lane == am, 1.0, 0.0), 0.0)
        ind = ind + hit
        g = jnp.where(lane == am, -jnp.inf, g)
    return ind


def _block_expand(j, kc):
    r_i = lax.broadcasted_iota(I32, (LANES, kc), 0)
    c_i = lax.broadcasted_iota(I32, (LANES, kc), 1)
    return jnp.where(r_i == (j * kc + c_i) // MOBA_BLOCK, 1.0, 0.0).astype(BF16)


def _flash_kernel(*refs, mode, tq, kc, nj, npairs):
    i = pl.program_id(0)
    j = pl.program_id(1)
    q_ref, k_ref, v_ref = refs[0], refs[1], refs[2]
    pos = 3
    if mode == "bias":
        b_ref = refs[pos]; pos += 1
    elif mode == "moba":
        km_ref = refs[pos]; pos += 1
    elif mode == "diff":
        lam_ref = refs[pos]; pos += 1
    o_ref = refs[pos]; pos += 1
    m_sc, l_sc, acc_sc = refs[pos], refs[pos + 1], refs[pos + 2]
    if mode == "moba":
        sel_sc = refs[pos + 3]
    jmax = ((i + 1) * tq - 1) // kc
    jdiag = (i * tq) // kc
    reps = kc // LANES
    nblk = kc // MOBA_BLOCK
    lane = lax.broadcasted_iota(I32, (tq, LANES), 1)
    lo = lane < 64

    def q_maps(p, scale):
        q2 = q_ref[:, p * LANES:(p + 1) * LANES]
        if scale != 1.0:
            q2 = q2 * scale
        return (jnp.where(lo, q2, 0.0).astype(BF16), jnp.where(lo, 0.0, q2).astype(BF16))

    @pl.when(j == 0)
    def _():
        m_sc[...] = jnp.full(m_sc.shape, NEG, F32)
        l_sc[...] = jnp.zeros(l_sc.shape, F32)
        acc_sc[...] = jnp.zeros(acc_sc.shape, F32)
        if mode == "moba":
            n_past = (i * tq + lax.broadcasted_iota(I32, (tq, 1), 0)) // MOBA_BLOCK
            for p in range(npairs):
                km2 = km_ref[:, p * LANES:(p + 1) * LANES].astype(BF16)
                for e, qm in enumerate(q_maps(p, 1.0)):
                    sel_sc[2 * p + e] = _moba_top_blocks(_dot_nt(qm, km2), n_past)

    def step(diag):
        if mode == "moba":
            row_t = i * tq + lax.broadcasted_iota(I32, (tq, MOBA_BLOCK), 0)
            lane_b = lax.broadcasted_iota(I32, (tq, MOBA_BLOCK), 1)
        if mode == "bias":
            bias = b_ref[0].astype(F32)
        elif diag and mode == "diff":
            causal = (j * kc + lax.broadcasted_iota(I32, (tq, kc), 1)
                      <= i * tq + lax.broadcasted_iota(I32, (tq, kc), 0))
        for p in range(npairs):
            k2 = k_ref[:, p * LANES:(p + 1) * LANES]
            v2 = v_ref[:, p * LANES:(p + 1) * LANES]
            pv, al = [], []
            for e, qm in enumerate(q_maps(p, Q_SCALE)):
                h = 2 * p + e
                s = _dot_nt(qm, k2)
                if mode == "bias":
                    s = s + bias
                elif mode == "moba":
                    sel = sel_sc[h]
                    parts = []
                    for c in range(nblk):
                        blk = j * nblk + c
                        sb = s[:, c * MOBA_BLOCK:(c + 1) * MOBA_BLOCK]
                        picked = jnp.sum(jnp.where(lane == blk, sel, 0.0), axis=1, keepdims=True)
                        sc = sb + jnp.where(picked > 0.5, 0.0, NEG)
                        if diag:
                            own = jnp.where(blk * MOBA_BLOCK + lane_b <= row_t, sb, NEG)
                            sc = jnp.where(row_t // MOBA_BLOCK == blk, own, sc)
                        parts.append(sc)
                    s = jnp.concatenate(parts, axis=1) if nblk > 1 else parts[0]
                elif diag:
                    s = jnp.where(causal, s, NEG)
                pr, alpha, m_new, l_new = _online_softmax(s, m_sc[h], l_sc[h], reps)
                m_sc[h] = m_new
                l_sc[h] = l_new
                pv.append(_dot(pr.astype(BF16), v2))
                al.append(alpha)
            if mode == "diff":
                acc_sc[2 * p] = acc_sc[2 * p] * al[0] + pv[0]
                acc_sc[2 * p + 1] = acc_sc[2 * p + 1] * al[1] + pv[1]
            else:
                acc_sc[p] = acc_sc[p] * jnp.where(lo, al[0], al[1]) + jnp.where(lo, pv[0], pv[1])

    if mode == "bias":
        pl.when(j <= jmax)(lambda: step(False))
    else:
        pl.when(j < jdiag)(lambda: step(False))
        pl.when(jnp.logical_and(j >= jdiag, j <= jmax))(lambda: step(True))

    @pl.when(j == nj - 1)
    def _():
        for p in range(npairs):
            if mode == "diff":
                o = acc_sc[2 * p] / l_sc[2 * p] - lam_ref[0, 0] * (acc_sc[2 * p + 1] / l_sc[2 * p + 1])
            else:
                o = acc_sc[p] / jnp.where(lo, l_sc[2 * p], l_sc[2 * p + 1])
            o_ref[:, p * LANES:(p + 1) * LANES] = o


def _flash_tiles(mode, S):
    tq, kc = {"bias": (512, 512), "moba": (512, 1024), "diff": (512, 1024)}[mode]
    return min(tq, S), min(kc, S)


def _flash(q, k_bf, v_bf, mode, bias=None, kmean=None, lam=None):
    S, W = q.shape
    tq, kc = _flash_tiles(mode, S)
    assert kc % MOBA_BLOCK == 0 and tq % MOBA_BLOCK == 0
    ni, nj = S // tq, S // kc
    npairs = W // LANES
    jm = lambda i, j: jnp.minimum(j, ((i + 1) * tq - 1) // kc)
    in_specs = [pl.BlockSpec((tq, W), lambda i, j: (i, 0)),
                pl.BlockSpec((kc, W), lambda i, j: (jm(i, j), 0)),
                pl.BlockSpec((kc, W), lambda i, j: (jm(i, j), 0))]
    args = [q, k_bf, v_bf]
    nmaps = 2 * npairs
    scratch = [pltpu.VMEM((nmaps, tq, LANES), F32), pltpu.VMEM((nmaps, tq, LANES), F32),
               pltpu.VMEM((nmaps if mode == "diff" else npairs, tq, LANES), F32)]
    if mode == "bias":
        in_specs.append(pl.BlockSpec((1, tq, kc), lambda i, j: (jm(i, j), i, 0)))
        args.append(bias)
    elif mode == "moba":
        in_specs.append(pl.BlockSpec(kmean.shape, lambda i, j: (0, 0)))
        args.append(kmean)
        scratch.append(pltpu.VMEM((nmaps, tq, LANES), F32))
    elif mode == "diff":
        in_specs.append(pl.BlockSpec(memory_space=pltpu.SMEM))
        args.append(lam.reshape(1, 1))
    return pl.pallas_call(
        functools.partial(_flash_kernel, mode=mode, tq=tq, kc=kc, nj=nj, npairs=npairs),
        grid=(ni, nj), in_specs=in_specs,
        out_specs=pl.BlockSpec((tq, W), lambda i, j: (i, 0)),
        out_shape=jax.ShapeDtypeStruct((S, W), F32), scratch_shapes=scratch,
        compiler_params=_cparams(("parallel", "arbitrary"),
                                 VMEM_LIMIT_LARGE if mode == "diff" else VMEM_LIMIT),
        name="flash_" + mode)(*args)


def _block_mean_kernel(k_ref, o_ref, *, nb, per):
    i = pl.program_id(0)

    @pl.when(i < nb // per)
    def _():
        x = k_ref[...]
        o_ref[...] = jnp.mean(x.reshape(per, MOBA_BLOCK, x.shape[1]), axis=1)

    @pl.when(i >= nb // per)
    def _():
        o_ref[...] = jnp.zeros(o_ref.shape, F32)


def _block_means(kb):
    S, W = kb.shape
    nb, per = S // MOBA_BLOCK, 8
    last = nb // per - 1
    return pl.pallas_call(
        functools.partial(_block_mean_kernel, nb=nb, per=per),
        grid=(LANES // per,),
        in_specs=[pl.BlockSpec((per * MOBA_BLOCK, W), lambda i: (jnp.minimum(i, last), 0))],
        out_specs=pl.BlockSpec((per, W), lambda i: (i, 0)),
        out_shape=jax.ShapeDtypeStruct((LANES, W), F32),
        compiler_params=_cparams(("arbitrary",)), name="block_means")(kb)


def _page_specs(shape, ppc, nj):
    zeros = (0,) * len(shape)

    def make(p):
        return pl.BlockSpec((1,) + tuple(shape),
                            lambda b, j, pt: (pt[b, jnp.minimum(j, nj - 1) * ppc + p],) + zeros)
    return [make(p) for p in range(ppc)]


def _pool_t(cache):
    NP = cache.shape[0]
    nd = cache.ndim
    return jnp.transpose(cache, (0,) + tuple(range(2, nd)) + (1,)).reshape(NP, -1, PAGE)


def _idx_scores_kernel(pt_ref, q_ref, w_ref, new_ref, *refs, ppc, nj, T):
    j = pl.program_id(1)
    pages, o_ref = refs[:ppc], refs[ppc]
    kc = ppc * PAGE
    q = q_ref[0].astype(BF16)
    w = w_ref[0]

    def scores(kt):
        d = jnp.maximum(_dot(q, kt), 0.0) * _tile_lanes(w, kt.shape[1] // LANES)
        return jnp.sum(d.reshape(T, N_IDX, kt.shape[1]), axis=1)

    @pl.when(j < nj)
    def _():
        kt = jnp.concatenate([r[0] for r in pages], axis=1).astype(BF16)
        o_ref[0, 0] = scores(kt)

    @pl.when(j == nj)
    def _():
        sc = scores(new_ref[0].astype(BF16))
        o_ref[0, 0] = jnp.concatenate([sc, jnp.zeros((T, kc - PAGE), F32)], axis=1)


def _idx_scores(pt, q_rows, w_rows, new_page, pool, ppc=4):
    B = q_rows.shape[0]
    T = q_rows.shape[1] // N_IDX
    nj = pt.shape[1] // ppc
    kc = ppc * PAGE
    grid_spec = pltpu.PrefetchScalarGridSpec(
        num_scalar_prefetch=1, grid=(B, nj + 1),
        in_specs=[pl.BlockSpec((1, T * N_IDX, HEAD_DIM), lambda b, j, pt: (b, 0, 0)),
                  pl.BlockSpec((1, T * N_IDX, LANES), lambda b, j, pt: (b, 0, 0)),
                  pl.BlockSpec((1, HEAD_DIM, PAGE), lambda b, j, pt: (b, 0, 0))]
        + _page_specs(pool.shape[1:], ppc, nj),
        out_specs=pl.BlockSpec((1, 1, T, kc), lambda b, j, pt: (j, b, 0, 0)))
    return pl.pallas_call(
        functools.partial(_idx_scores_kernel, ppc=ppc, nj=nj, T=T),
        grid_spec=grid_spec, out_shape=jax.ShapeDtypeStruct((nj + 1, B, T, kc), F32),
        compiler_params=_cparams(("parallel", "arbitrary")), name="idx_scores",
    )(pt, q_rows, w_rows, new_page, *([pool] * ppc))


def _pool_block_mean_kernel(pt_ref, *refs, npg):
    j = pl.program_id(1)
    pages, o_ref = refs[:npg], refs[npg]
    ppb = MOBA_BLOCK // PAGE
    lane = lax.broadcasted_iota(I32, o_ref.shape[1:], 1)

    @pl.when(j == 0)
    def _():
        o_ref[...] = jnp.zeros(o_ref.shape, F32)

    acc = o_ref[0]
    for blk in range(npg // ppb):
        tot = pages[blk * ppb][0]
        for a in range(1, ppb):
            tot = tot + pages[blk * ppb + a][0]
        mean = jnp.sum(tot, axis=1, keepdims=True) * (1.0 / MOBA_BLOCK)
        acc = jnp.where(lane == j * (npg // ppb) + blk, mean, acc)
    o_ref[0] = acc


def _pool_block_means(pt, pool_t, bpc=8):
    B, n_pages = pt.shape
    W = pool_t.shape[1]
    ppb = MOBA_BLOCK // PAGE
    nb = n_pages // ppb
    assert nb <= LANES
    bpc = min(bpc, nb)
    npg = bpc * ppb
    nj = nb // bpc
    grid_spec = pltpu.PrefetchScalarGridSpec(
        num_scalar_prefetch=1, grid=(B, nj),
        in_specs=_page_specs((W, PAGE), npg, nj),
        out_specs=pl.BlockSpec((1, W, LANES), lambda b, j, pt: (b, 0, 0)))
    return pl.pallas_call(
        functools.partial(_pool_block_mean_kernel, npg=npg),
        grid_spec=grid_spec, out_shape=jax.ShapeDtypeStruct((B, W, LANES), F32),
        compiler_params=_cparams(("parallel", "arbitrary")), name="pool_block_means",
    )(pt, *([pool_t] * npg))


def _paged_attn_kernel(pt_ref, *refs, mode, ppc, nj, T, G, W, Wv, vt):
    j = pl.program_id(1)
    q_ref, knew_ref, vnew_ref = refs[0], refs[1], refs[2]
    pos = 3
    if mode == "bias":
        b_ref = refs[pos]; pos += 1
    elif mode == "moba":
        km_ref = refs[pos]; pos += 1
    kpages = refs[pos:pos + ppc]; pos += ppc
    vpages = refs[pos:pos + ppc]; pos += ppc
    o_ref = refs[pos]; pos += 1
    m_sc, l_sc, acc_sc = refs[pos], refs[pos + 1], refs[pos + 2]
    if mode == "moba":
        sel_sc = refs[pos + 3]
    R = T * G
    kc = ppc * PAGE
    row = lax.broadcasted_iota(I32, (R, W), 0)
    col = lax.broadcasted_iota(I32, (R, W), 1)
    q = q_ref[0]
    qrep = jnp.concatenate([jnp.broadcast_to(q[t:t + 1], (G, W)) for t in range(T)], axis=0)
    qraw = jnp.where(col // HEAD_DIM == row % G, qrep, 0.0)
    qbd = (qraw * Q_SCALE).astype(BF16)

    @pl.when(j == 0)
    def _():
        m_sc[...] = jnp.full(m_sc.shape, NEG, F32)
        l_sc[...] = jnp.zeros(l_sc.shape, F32)
        acc_sc[...] = jnp.zeros(acc_sc.shape, F32)
        if mode == "moba":
            n_full = (nj * kc) // MOBA_BLOCK
            gate = _dot(qraw.astype(BF16), km_ref[0].astype(BF16))
            sel_sc[...] = _moba_top_blocks(gate, n_full).astype(BF16)

    def update(s, v):
        pr, alpha, m_new, l_new = _online_softmax(s, m_sc[...], l_sc[...], s.shape[1] // LANES)
        m_sc[...] = m_new
        l_sc[...] = l_new
        pv = _dot_nt(pr.astype(BF16), v) if vt else _dot(pr.astype(BF16), v)
        acc_sc[...] = acc_sc[...] * _tile_lanes(alpha, Wv // LANES) + pv

    @pl.when(j < nj)
    def _():
        k = jnp.concatenate([r[0] for r in kpages], axis=1).astype(BF16)
        v = jnp.concatenate([r[0] for r in vpages], axis=1 if vt else 0).astype(BF16)
        s = _dot(qbd, k)
        if mode == "bias":
            b = b_ref[0, 0].astype(F32)
            s = s + jnp.concatenate([jnp.broadcast_to(b[t:t + 1], (G, kc)) for t in range(T)], axis=0)
        elif mode == "moba":
            s = jnp.where(_dot(sel_sc[...], _block_expand(j, kc)) > 0.5, s, NEG)
        update(s, v)

    @pl.when(j == nj)
    def _():
        s = _dot(qbd, knew_ref[0].astype(BF16))
        if mode == "bias":
            b = b_ref[0, 0][:, :PAGE].astype(F32)
            s = s + jnp.concatenate([jnp.broadcast_to(b[t:t + 1], (G, PAGE)) for t in range(T)], axis=0)
        else:
            r_i = lax.broadcasted_iota(I32, (R, PAGE), 0)
            c_i = lax.broadcasted_iota(I32, (R, PAGE), 1)
            s = jnp.where(c_i <= r_i // G, s, NEG)
        update(s, vnew_ref[0].astype(BF16))
        o = acc_sc[...] / _tile_lanes(l_sc[...], Wv // LANES)
        rv = lax.broadcasted_iota(I32, (R, Wv), 0)
        cv = lax.broadcasted_iota(I32, (R, Wv), 1)
        o = jnp.where(cv // HEAD_DIM == rv % G, o, 0.0)
        o_ref[0] = jnp.sum(o.reshape(T, G, Wv), axis=1)


def _paged_diff_kernel(pt_ref, q_ref, knew_ref, vnew_ref, lam_ref, *refs, ppc, nj, T, H, W):
    j = pl.program_id(1)
    kpages, vpages = refs[:ppc], refs[ppc:2 * ppc]
    o_ref, m_sc, l_sc, acc_sc = refs[2 * ppc:2 * ppc + 4]
    G = 2 * H
    R = G * T
    RH = 2 * T
    row = lax.broadcasted_iota(I32, (R, W), 0)
    col = lax.broadcasted_iota(I32, (R, W), 1)
    qrep = jnp.concatenate([q_ref[0]] * G, axis=0)
    qbd = (jnp.where(col // HEAD_DIM == row // T, qrep, 0.0) * Q_SCALE).astype(BF16)

    @pl.when(j == 0)
    def _():
        m_sc[...] = jnp.full(m_sc.shape, NEG, F32)
        l_sc[...] = jnp.zeros(l_sc.shape, F32)
        acc_sc[...] = jnp.zeros(acc_sc.shape, F32)

    def update(s, vrefs):
        pr, alpha, m_new, l_new = _online_softmax(s, m_sc[...], l_sc[...], s.shape[1] // LANES)
        m_sc[...] = m_new
        l_sc[...] = l_new
        prb = pr.astype(BF16)
        for h in range(H):
            vh = jnp.concatenate([r[0, :, h, :] for r in vrefs], axis=0).astype(BF16)
            rows = slice(h * RH, (h + 1) * RH)
            acc_sc[rows, :] = acc_sc[rows, :] * alpha[rows, :] + _dot(prb[rows, :], vh)

    @pl.when(j < nj)
    def _():
        k = jnp.concatenate([r[0] for r in kpages], axis=1).astype(BF16)
        update(_dot(qbd, k), vpages)

    @pl.when(j == nj)
    def _():
        s = _dot(qbd, knew_ref[0].astype(BF16))
        r_i = lax.broadcasted_iota(I32, (R, PAGE), 0)
        c_i = lax.broadcasted_iota(I32, (R, PAGE), 1)
        update(jnp.where(c_i <= r_i % T, s, NEG), [vnew_ref])
        o = acc_sc[...] / l_sc[...]
        outs = []
        for h in range(H):
            outs.append(o[h * RH:h * RH + T, :] - lam_ref[0, 0] * o[h * RH + T:(h + 1) * RH, :])
        o_ref[0] = jnp.concatenate(outs, axis=1)


def _paged_diff(pt, q3, knew, vnew, pool_k, pool_v, lam, ppc=4):
    B, T, W = q3.shape
    H, E = pool_v.shape[2], pool_v.shape[3]
    assert E == LANES and W == 2 * H * HEAD_DIM and (2 * T) % 8 == 0
    nj = pt.shape[1] // ppc
    cmap3 = lambda b, j, pt: (b, 0, 0)
    cmap4 = lambda b, j, pt: (b, 0, 0, 0)
    R = 2 * H * T
    grid_spec = pltpu.PrefetchScalarGridSpec(
        num_scalar_prefetch=1, grid=(B, nj + 1),
        in_specs=[pl.BlockSpec((1, T, W), cmap3), pl.BlockSpec((1, W, PAGE), cmap3),
                  pl.BlockSpec((1, PAGE, H, E), cmap4), pl.BlockSpec(memory_space=pltpu.SMEM)]
        + _page_specs(pool_k.shape[1:], ppc, nj) + _page_specs(pool_v.shape[1:], ppc, nj),
        out_specs=pl.BlockSpec((1, T, H * E), cmap3),
        scratch_shapes=[pltpu.VMEM((R, LANES), F32), pltpu.VMEM((R, LANES), F32), pltpu.VMEM((R, E), F32)])
    return pl.pallas_call(
        functools.partial(_paged_diff_kernel, ppc=ppc, nj=nj, T=T, H=H, W=W),
        grid_spec=grid_spec, out_shape=jax.ShapeDtypeStruct((B, T, H * E), F32),
        compiler_params=_cparams(("parallel", "arbitrary")), name="paged_diff",
    )(pt, q3, knew, vnew, lam.reshape(1, 1), *([pool_k] * ppc), *([pool_v] * ppc))


def _paged_attn(pt, q3, knew, vnew, pool_k, pool_v, mode, vt, bias=None, kmean=None, ppc=4):
    B, T, W = q3.shape
    Wv = pool_v.shape[1] if vt else pool_v.shape[2]
    G = W // HEAD_DIM
    nj = pt.shape[1] // ppc
    kc = ppc * PAGE
    cmap = lambda b, j, pt: (b, 0, 0)
    in_specs = [pl.BlockSpec((1, T, W), cmap), pl.BlockSpec((1,) + knew.shape[1:], cmap),
                pl.BlockSpec((1,) + vnew.shape[1:], cmap)]
    args = [q3, knew, vnew]
    scratch = [pltpu.VMEM((T * G, LANES), F32), pltpu.VMEM((T * G, LANES), F32),
               pltpu.VMEM((T * G, Wv), F32)]
    if mode == "bias":
        in_specs.append(pl.BlockSpec((1, 1, T, kc), lambda b, j, pt: (j, b, 0, 0)))
        args.append(bias)
    elif mode == "moba":
        in_specs.append(pl.BlockSpec((1, W, LANES), cmap))
        args.append(kmean)
        scratch.append(pltpu.VMEM((T * G, LANES), BF16))
    in_specs += _page_specs(pool_k.shape[1:], ppc, nj) + _page_specs(pool_v.shape[1:], ppc, nj)
    args += [pool_k] * ppc + [pool_v] * ppc
    grid_spec = pltpu.PrefetchScalarGridSpec(
        num_scalar_prefetch=1, grid=(B, nj + 1), in_specs=in_specs,
        out_specs=pl.BlockSpec((1, T, Wv), cmap), scratch_shapes=scratch)
    return pl.pallas_call(
        functools.partial(_paged_attn_kernel, mode=mode, ppc=ppc, nj=nj, T=T, G=G, W=W, Wv=Wv, vt=vt),
        grid_spec=grid_spec, out_shape=jax.ShapeDtypeStruct((B, T, Wv), F32),
        compiler_params=_cparams(("parallel", "arbitrary")), name="paged_" + mode)(pt, *args)


def _lambda_kernel(a_ref, o_ref, *, lam_init):
    a = a_ref[...]
    d1 = jnp.sum(a[0:1] * a[1:2], axis=1, keepdims=True)
    d2 = jnp.sum(a[2:3] * a[3:4], axis=1, keepdims=True)
    o_ref[...] = jnp.exp(d1) - jnp.exp(d2) + lam_init


def _diff_lambda(lq1, lk1, lq2, lk2, lam_init):
    a = jnp.zeros((8, LANES), F32).at[:4, :HEAD_DIM].set(jnp.stack([lq1, lk1, lq2, lk2]))
    return pl.pallas_call(functools.partial(_lambda_kernel, lam_init=lam_init),
                          out_shape=jax.ShapeDtypeStruct((1, 1), F32), name="diff_lambda")(a)


def _pad_new_page(x3):
    B, T, W = x3.shape
    return jnp.pad(x3, ((0, 0), (0, PAGE - T), (0, 0)))


def _new_page_t(x3):
    B, T, W = x3.shape
    return jnp.pad(jnp.swapaxes(x3, 1, 2), ((0, 0), (0, 0), (0, PAGE - T)))


def kernel(x_prompt, x_sample, cache_a_k, cache_a_v, cache_idx_k, cache_b_k, cache_b_v, cache_c_k, cache_c_v, cache_mem_k, cache_mem_v, page_table, mem_prompt, g_mix, g_mem, g_final, w_in_even, w_out_even, w_in_odd, w_out_odd, lam_q1, lam_k1, lam_q2, lam_k2, g_subln, w_mq, w_mk, w_mv, w_mo):
    Bp, S, D = x_prompt.shape
    Bs, T, _ = x_sample.shape
    assert Bp == 1
    depth = g_mix.shape[0]
    n_pages = page_table.shape[1]
    past = n_pages * PAGE
    assert past % MOBA_BLOCK == 0 and S % MOBA_BLOCK == 0
    NP = cache_a_k.shape[1]
    HA = cache_a_k.shape[3]
    HB = cache_b_k.shape[3]
    HC = cache_c_k.shape[3]
    HM, HDM = cache_mem_k.shape[3], cache_mem_k.shape[4]
    NM = cache_mem_k.shape[2]
    WA, WB, WC, WM = HA * HEAD_DIM, HB * HEAD_DIM, HC * 2 * HEAD_DIM, HM * HDM
    WIQ = N_IDX * HEAD_DIM
    Ms = Bs * T

    cos_p, sin_p = _rope_tables(jnp.arange(S))
    cos_s, sin_s = _rope_tables(jnp.tile(past + jnp.arange(T), Bs))
    xp = x_prompt.reshape(S, D)
    xs = x_sample.reshape(Ms, D)
    mem2 = mem_prompt.reshape(NM, D)
    pt = page_table.astype(I32)

    n_tail = 4 * WA + 4 * WB + WIQ
    even_groups = []
    for gi in range(8):
        kind = "rope" if gi in (0, 1, 4, 5) else "plain"
        even_groups.append((gi * WA, WA, kind, 1.0, gi in (1, 2, 5, 6)))
    even_groups.append((8 * WA, WIQ, "rope", 1.0, False))
    even_groups.append((n_tail, LANES, "tail", float(WIQ) ** -0.5, True))
    odd_groups = [(0, WC, "rope", 1.0, False), (WC, WC, "rope", 1.0, True),
                  (2 * WC, WC, "plain", 1.0, True), (3 * WC, WC, "plain", 1.0, False)]
    mem_groups = [(0, WM, "plain", 1.0, False), (WM, WM, "plain", 1.0, False)]

    outs = {}
    ones = jnp.ones((D,), F32)
    for l in range(depth):
        i = l // 2
        if l % 2 == 0:
            w_in = w_in_even[i]
            w_in = jnp.pad(w_in, ((0, 0), (0, n_tail + LANES - w_in.shape[1]))).astype(BF16)
            w_out = w_out_even[i].astype(BF16)
            (qa, ka, ka_b, va, va_b, ga, qb, kb, kb_b, vb, vb_b, gb, qi, tail, tail_b) = _rms_proj(
                xp, g_mix[l], w_in, even_groups, cos_p, sin_p)
            kk = min(TOPK_TOK, S // 4)
            bias = _dsa_select(qi, tail, tail_b, kk)
            oa = _flash(qa, ka_b, va_b, "bias", bias=bias)
            ob = _flash(qb, kb_b, vb_b, "moba", kmean=_block_means(kb))
            xp = _gated_out(xp, w_out, [(oa, ga), (ob, gb)])
            outs.setdefault("pa_k", []).append(ka.reshape(1, S, HA, HEAD_DIM))
            outs.setdefault("pa_v", []).append(va.reshape(1, S, HA, HEAD_DIM))
            outs.setdefault("pidx_k", []).append(tail[:, :HEAD_DIM].reshape(1, S, HEAD_DIM))
            outs.setdefault("pb_k", []).append(kb.reshape(1, S, HB, HEAD_DIM))
            outs.setdefault("pb_v", []).append(vb.reshape(1, S, HB, HEAD_DIM))
            (qa, ka, _, va, _, ga, qb, kb, _, vb, _, gb, qi, tail, _) = _rms_proj(
                xs, g_mix[l], w_in, even_groups, cos_s, sin_s)
            q_rows = qi.reshape(Bs, T * N_IDX, HEAD_DIM)
            w_rows = jnp.broadcast_to(tail[:, HEAD_DIM:HEAD_DIM + N_IDX].reshape(Bs, T * N_IDX, 1),
                                      (Bs, T * N_IDX, LANES))
            tail3 = tail.reshape(Bs, T, LANES)
            ki_new = _new_page_t(tail3[:, :, :HEAD_DIM])
            sc = _idx_scores(pt, q_rows, w_rows, ki_new, _pool_t(cache_idx_k[i]))
            ncs, _, _, kcs = sc.shape
            lim = (past + jnp.tile(jnp.arange(T, dtype=I32), Bs)).reshape(Ms, 1)
            bias_s = _select_rows(sc.reshape(ncs, Ms, kcs), lim, min(TOPK_TOK, (past + T) // 4))
            oa = _paged_attn(pt, qa.reshape(Bs, T, WA), _new_page_t(ka.reshape(Bs, T, WA)),
                             _new_page_t(va.reshape(Bs, T, WA)),
                             _pool_t(cache_a_k[i]), _pool_t(cache_a_v[i]),
                             "bias", True, bias=bias_s.reshape(ncs, Bs, T, kcs))
            pool_bk = _pool_t(cache_b_k[i])
            ob = _paged_attn(pt, qb.reshape(Bs, T, WB), _new_page_t(kb.reshape(Bs, T, WB)),
                             _new_page_t(vb.reshape(Bs, T, WB)),
                             pool_bk, _pool_t(cache_b_v[i]),
                             "moba", True, kmean=_pool_block_means(pt, pool_bk))
            xs = _gated_out(xs, w_out, [(oa.reshape(Ms, WA), ga), (ob.reshape(Ms, WB), gb)])
            outs.setdefault("sa_k", []).append(ka.reshape(Bs, T, HA, HEAD_DIM))
            outs.setdefault("sa_v", []).append(va.reshape(Bs, T, HA, HEAD_DIM))
            outs.setdefault("sidx_k", []).append(tail[:, :HEAD_DIM].reshape(Bs, T, HEAD_DIM))
            outs.setdefault("sb_k", []).append(kb.reshape(Bs, T, HB, HEAD_DIM))
            outs.setdefault("sb_v", []).append(vb.reshape(Bs, T, HB, HEAD_DIM))
        else:
            lam_init = 0.8 - 0.6 * math.exp(-0.3 * l)
            lam = _diff_lambda(lam_q1[i], lam_k1[i], lam_q2[i], lam_k2[i], lam_init)
            w_in = w_in_odd[i].astype(BF16)
            w_out = w_out_odd[i].astype(BF16)
            q, k, k_b, v, v_b, g = _rms_proj(xp, g_mix[l], w_in, odd_groups, cos_p, sin_p)
            o = _flash(q, k_b, v_b, "diff", lam=lam)
            xp = _gated_out(xp, w_out, [(o, g)], gsub=g_subln[i], post_scale=1.0 - lam_init)
            outs.setdefault("pc_k", []).append(k.reshape(1, S, HC, 2, HEAD_DIM))
            outs.setdefault("pc_v", []).append(v.reshape(1, S, HC, 2 * HEAD_DIM))
            q, k, _, v, _, g = _rms_proj(xs, g_mix[l], w_in, odd_groups, cos_s, sin_s)
            o = _paged_diff(pt, q.reshape(Bs, T, WC), _new_page_t(k.reshape(Bs, T, WC)),
                            _pad_new_page(v.reshape(Bs, T, WC)).reshape(Bs, PAGE, HC, 2 * HEAD_DIM),
                            _pool_t(cache_c_k[i]), cache_c_v[i], lam)
            xs = _gated_out(xs, w_out, [(o.reshape(Ms, WC), g)], gsub=g_subln[i], post_scale=1.0 - lam_init)
            outs.setdefault("sc_k", []).append(k.reshape(Bs, T, HC, 2, HEAD_DIM))
            outs.setdefault("sc_v", []).append(v.reshape(Bs, T, HC, 2 * HEAD_DIM))
        w_kv = jnp.concatenate([w_mk[l], w_mv[l]], axis=1).astype(BF16)
        mk, mv = _rms_proj(mem2, ones, w_kv, mem_groups, norm=False)
        w_q = w_mq[l].astype(BF16)
        w_o = w_mo[l].astype(BF16)
        last = l == depth - 1
        q, g = _rms_proj(xp, g_mem[l], w_q, mem_groups)
        o = _mem_attn(q.reshape(1, S, WM), mk.reshape(1, NM, WM), mv.reshape(1, NM, WM), HM, HDM)
        res = _gated_out(xp, w_o, [(o.reshape(S, WM), g)], gfin=g_final if last else None)
        xp, yp = res if last else (res, None)
        q, g = _rms_proj(xs, g_mem[l], w_q, mem_groups)
        o = _mem_attn(q.reshape(Bs, T, WM), cache_mem_k[l].reshape(Bs, NM, WM),
                      cache_mem_v[l].reshape(Bs, NM, WM), HM, HDM)
        res = _gated_out(xs, w_o, [(o.reshape(Ms, WM), g)], gfin=g_final if last else None)
        xs, ys = res if last else (res, None)
        outs.setdefault("pm_k", []).append(mk.reshape(1, NM, HM, HDM))
        outs.setdefault("pm_v", []).append(mv.reshape(1, NM, HM, HDM))

    st = lambda name: jnp.stack(outs[name])
    return (yp.reshape(1, S, D), ys.reshape(Bs, T, D),
            st("pa_k"), st("pa_v"), st("pidx_k"), st("pb_k"), st("pb_v"), st("pc_k"), st("pc_v"),
            st("pm_k"), st("pm_v"),
            st("sa_k"), st("sa_v"), st("sidx_k"), st("sb_k"), st("sb_v"), st("sc_k"), st("sc_v"))
```

```python
import functools
import math

import jax
import jax.numpy as jnp
from jax import lax
from jax.experimental import pallas as pl
from jax.experimental.pallas import tpu as pltpu

F32 = jnp.float32
BF16 = jnp.bfloat16
I32 = jnp.int32

HEAD_DIM = 64
PAGE = 128
N_IDX = 8
TOPK_TOK = 256
MOBA_BLOCK = 256
MOBA_TOPK = 3
ROPE_THETA = 10000.0
EPS = 1e-6
LANES = 128
NEG = -1e30
Q_SCALE = HEAD_DIM ** -0.5 * math.log2(math.e)
INT_MIN = -2147483648
KEY_NEG_INF = -2139095041
VMEM_LIMIT = 56 * 1024 * 1024
VMEM_LIMIT_LARGE = 60 * 1024 * 1024


def _cparams(sem, vmem=VMEM_LIMIT):
    return pltpu.CompilerParams(dimension_semantics=sem, vmem_limit_bytes=vmem)


def _dot_nt(a, b):
    return lax.dot_general(a, b, (((1,), (1,)), ((), ())), preferred_element_type=F32)


def _dot(a, b):
    return jnp.dot(a, b, preferred_element_type=F32)


def _tile_lanes(x, reps):
    return x if reps == 1 else jnp.concatenate([x] * reps, axis=1)


def _rope_tables(pos):
    inv = ROPE_THETA ** (-jnp.arange(0, HEAD_DIM, 2, dtype=F32) / HEAD_DIM)
    ang = pos.astype(F32)[:, None] * inv[None, :]
    cos, sin = jnp.cos(ang), jnp.sin(ang)
    cos128 = jnp.concatenate([cos] * 4, axis=1)
    sin128 = jnp.concatenate([-sin, sin, -sin, sin], axis=1)
    return cos128, sin128


def _rope128(z, c, s):
    lane = lax.broadcasted_iota(I32, z.shape, 1)
    first = (lane & 63) < 32
    partner = jnp.where(first, pltpu.roll(z, 96, axis=1), pltpu.roll(z, 32, axis=1))
    return z * c + partner * s


def _proj_kernel(*refs, groups, norm, has_rope):
    x_ref, g_ref, w_ref = refs[0], refs[1], refs[2]
    pos = 3
    if has_rope:
        c = refs[3][...]
        s = refs[4][...]
        pos = 5
    outs = refs[pos:]
    x = x_ref[...]
    if norm:
        x = x * lax.rsqrt(jnp.mean(x * x, axis=-1, keepdims=True) + EPS) * g_ref[...]
    xb = x.astype(BF16)
    oi = 0
    for start, width, kind, scale, dup in groups:
        z = _dot(xb, w_ref[:, start:start + width])
        if kind == "rope":
            z = jnp.concatenate(
                [_rope128(z[:, a:a + LANES], c, s) for a in range(0, width, LANES)], axis=1)
        elif kind == "tail":
            lane = lax.broadcasted_iota(I32, z.shape, 1)
            z = jnp.where(lane < 64, _rope128(z, c, s),
                          jnp.where(lane < 64 + N_IDX, z * scale, 0.0))
        outs[oi][...] = z
        oi += 1
        if dup:
            outs[oi][...] = z.astype(BF16)
            oi += 1


def _rms_proj(x, g, w, groups, cos=None, sin=None, norm=True):
    M, D = x.shape
    tm = min(256, M)
    has_rope = cos is not None
    in_specs = [pl.BlockSpec((tm, D), lambda i: (i, 0)),
                pl.BlockSpec((1, D), lambda i: (0, 0)),
                pl.BlockSpec(w.shape, lambda i: (0, 0))]
    args = [x, g.reshape(1, D), w]
    if has_rope:
        in_specs += [pl.BlockSpec((tm, LANES), lambda i: (i, 0))] * 2
        args += [cos, sin]
    out_shape, out_specs = [], []
    for _, width, _, _, dup in groups:
        out_shape.append(jax.ShapeDtypeStruct((M, width), F32))
        out_specs.append(pl.BlockSpec((tm, width), lambda i: (i, 0)))
        if dup:
            out_shape.append(jax.ShapeDtypeStruct((M, width), BF16))
            out_specs.append(pl.BlockSpec((tm, width), lambda i: (i, 0)))
    return pl.pallas_call(
        functools.partial(_proj_kernel, groups=tuple(groups), norm=norm, has_rope=has_rope),
        grid=(M // tm,), in_specs=in_specs, out_specs=out_specs, out_shape=out_shape,
        compiler_params=_cparams(("parallel",)), name="rms_proj")(*args)


def _silu(g):
    return g * (1.0 / (1.0 + jnp.exp(-g)))


def _gated_out_kernel(*refs, widths, subnorm, post_scale, final_norm):
    resid_ref, w_ref = refs[0], refs[1]
    pos = 2
    if subnorm:
        gsub = refs[pos][...]
        pos += 1
    if final_norm:
        gfin = refs[pos][...]
        pos += 1
    acc = resid_ref[...]
    off = 0
    for wd in widths:
        o = refs[pos][...]
        g = refs[pos + 1][...]
        pos += 2
        if subnorm:
            parts = []
            for a in range(0, wd, LANES):
                oh = o[:, a:a + LANES]
                parts.append(oh * lax.rsqrt(jnp.mean(oh * oh, axis=-1, keepdims=True) + EPS)
                             * gsub * post_scale)
            o = jnp.concatenate(parts, axis=1)
        z = (o * _silu(g)).astype(BF16)
        acc = acc + _dot(z, w_ref[off:off + wd, :])
        off += wd
    out_ref = refs[pos]
    out_ref[...] = acc
    if final_norm:
        refs[pos + 1][...] = acc * lax.rsqrt(jnp.mean(acc * acc, axis=-1, keepdims=True) + EPS) * gfin


def _gated_out(resid, w, parts, gsub=None, post_scale=1.0, gfin=None):
    M, D = resid.shape
    tm = min(256, M)
    widths = tuple(o.shape[1] for o, _ in parts)
    in_specs = [pl.BlockSpec((tm, D), lambda i: (i, 0)), pl.BlockSpec(w.shape, lambda i: (0, 0))]
    args = [resid, w]
    if gsub is not None:
        in_specs.append(pl.BlockSpec((1, LANES), lambda i: (0, 0)))
        args.append(gsub.reshape(1, LANES))
    if gfin is not None:
        in_specs.append(pl.BlockSpec((1, D), lambda i: (0, 0)))
        args.append(gfin.reshape(1, D))
    for o, g in parts:
        wd = o.shape[1]
        in_specs += [pl.BlockSpec((tm, wd), lambda i: (i, 0))] * 2
        args += [o, g]
    out_shape = [jax.ShapeDtypeStruct((M, D), F32)]
    out_specs = [pl.BlockSpec((tm, D), lambda i: (i, 0))]
    if gfin is not None:
        out_shape.append(jax.ShapeDtypeStruct((M, D), F32))
        out_specs.append(pl.BlockSpec((tm, D), lambda i: (i, 0)))
    res = pl.pallas_call(
        functools.partial(_gated_out_kernel, widths=widths, subnorm=gsub is not None,
                          post_scale=post_scale, final_norm=gfin is not None),
        grid=(M // tm,), in_specs=in_specs, out_specs=out_specs, out_shape=out_shape,
        compiler_params=_cparams(("parallel",)), name="gated_out")(*args)
    return res if gfin is not None else res[0]


def _mem_attn_kernel(q_ref, k_ref, v_ref, o_ref, *, heads, hd):
    scale = hd ** -0.5
    outs = []
    for h in range(heads):
        q = q_ref[0, :, h * hd:(h + 1) * hd].astype(BF16)
        k = k_ref[0, :, h * hd:(h + 1) * hd].astype(BF16)
        v = v_ref[0, :, h * hd:(h + 1) * hd].astype(BF16)
        s = _dot_nt(q, k) * scale
        m = jnp.max(s, axis=1, keepdims=True)
        p = jnp.exp(s - m)
        l = jnp.sum(p, axis=1, keepdims=True)
        outs.append(_dot((p / l).astype(BF16), v))
    o_ref[0] = jnp.concatenate(outs, axis=1)


def _mem_attn(q3, mk3, mv3, heads, hd):
    B, T, W = q3.shape
    Bk, N, _ = mk3.shape
    tq = min(256, T)
    kidx = (lambda b, i: (b, 0, 0)) if Bk == B else (lambda b, i: (0, 0, 0))
    return pl.pallas_call(
        functools.partial(_mem_attn_kernel, heads=heads, hd=hd),
        grid=(B, T // tq),
        in_specs=[pl.BlockSpec((1, tq, W), lambda b, i: (b, i, 0)),
                  pl.BlockSpec((1, N, W), kidx), pl.BlockSpec((1, N, W), kidx)],
        out_specs=pl.BlockSpec((1, tq, W), lambda b, i: (b, i, 0)),
        out_shape=jax.ShapeDtypeStruct((B, T, W), F32),
        compiler_params=_cparams(("parallel", "parallel")), name="mem_attn")(q3, mk3, mv3)


def _float_key(x):
    bits = lax.bitcast_convert_type(x, I32)
    return jnp.where(bits < 0, bits ^ 0x7FFFFFFF, bits)


def _topk_bias(key_ref, n, out_ref, lim, *, kk, R, kc):
    slabs = kc // LANES

    def count(thr_b, strict):
        def body(c, acc):
            kt = key_ref[c]
            for a in range(slabs):
                blk = kt[:, a * LANES:(a + 1) * LANES]
                hit = (blk > thr_b) if strict else (blk >= thr_b)
                acc = acc + jnp.where(hit, 1, 0)
            return acc
        acc = lax.fori_loop(0, n, body, jnp.zeros((R, LANES), I32))
        return jnp.sum(acc, axis=1, keepdims=True)

    def search(state):
        b, t_u, done, _ = state
        cand_u = t_u | jnp.left_shift(jnp.int32(1), 31 - b)
        cnt = count(cand_u ^ INT_MIN, False)
        t_new = jnp.where(done > 0, t_u, jnp.where(cnt >= kk, cand_u, t_u))
        done = jnp.where(cnt == kk, 1, done)
        return b + 1, t_new, done, jnp.min(done)

    zeros = jnp.zeros((R, LANES), I32)
    _, t_u, _, _ = lax.while_loop(lambda st: jnp.logical_and(st[0] < 32, st[3] == 0), search,
                                  (jnp.int32(0), zeros, zeros, jnp.int32(0)))
    thr = t_u ^ INT_MIN
    cnt_gt = count(thr, True)
    need = kk - cnt_gt
    n_eq = count(thr, False) - cnt_gt
    surplus = jnp.where(n_eq > need, jnp.where(thr[:, :1] > KEY_NEG_INF, 1, 0), 0)
    has_ties = jnp.max(surplus)
    thr_full = _tile_lanes(thr, slabs)
    lane = lax.broadcasted_iota(I32, (R, kc), 1)

    @pl.when(has_ties == 0)
    def _():
        def body(c, carry):
            sel = jnp.where(key_ref[c] >= thr_full, 0.0, NEG)
            out_ref[c] = jnp.where(c * kc + lane <= lim, sel, NEG).astype(out_ref.dtype)
            return carry
        lax.fori_loop(0, n, body, 0)

    @pl.when(has_ties != 0)
    def _():
        r_i = lax.broadcasted_iota(I32, (kc, kc), 0)
        c_i = lax.broadcasted_iota(I32, (kc, kc), 1)
        upper = jnp.where(r_i <= c_i, 1.0, 0.0).astype(BF16)
        need_f = need.astype(F32)

        def body(c, run):
            kt = key_ref[c]
            eq = jnp.where(kt == thr_full, 1.0, 0.0)
            rank = run + _dot(eq.astype(BF16), upper)
            take_eq = jnp.where(rank <= need_f, eq, 0.0)
            sel = jnp.where(kt > thr_full, 0.0, jnp.where(take_eq > 0.5, 0.0, NEG))
            out_ref[c] = jnp.where(c * kc + lane <= lim, sel, NEG).astype(out_ref.dtype)
            return run + jnp.sum(eq, axis=1, keepdims=True)
        lax.fori_loop(0, n, body, jnp.zeros((R, 1), F32))


def _dsa_select_kernel(qi_ref, wq_ref, kt_ref, out_ref, key_sc, qh_sc, wb_sc, *, tq, kc, nc, kk):
    i = pl.program_id(0)
    t0 = i * tq
    n = (t0 + tq + kc - 1) // kc
    lane = lax.broadcasted_iota(I32, (tq, LANES), 1)
    q = qi_ref[...]
    wq = wq_ref[...]
    for h in range(N_IDX):
        slab = q[:, (h // 2) * LANES:(h // 2 + 1) * LANES]
        if h % 2 == 1:
            slab = pltpu.roll(slab, 64, axis=1)
        qh_sc[h] = jnp.where(lane < 64, slab, 0.0).astype(BF16)
        wcol = jnp.sum(jnp.where(lane == 64 + h, wq, 0.0), axis=1, keepdims=True)
        wb_sc[h] = jnp.broadcast_to(wcol, (tq, LANES))
    row_t = t0 + lax.broadcasted_iota(I32, (tq, 1), 0)
    lane_k = lax.broadcasted_iota(I32, (tq, kc), 1)
    slabs = kc // LANES

    def chunk(c, carry):
        kt = kt_ref[pl.ds(pl.multiple_of(c * kc, kc), kc), :]
        acc = jnp.zeros((tq, kc), F32)
        for h in range(N_IDX):
            d = _dot_nt(qh_sc[h], kt)
            acc = acc + _tile_lanes(wb_sc[h], slabs) * jnp.maximum(d, 0.0)
        sc = jnp.where(c * kc + lane_k <= row_t, acc, -jnp.inf)
        key_sc[c] = _float_key(sc)
        return carry
    lax.fori_loop(0, n, chunk, 0)

    _topk_bias(key_sc, n, out_ref, row_t, kk=kk, R=tq, kc=kc)

    def fill(c, carry):
        out_ref[c] = jnp.full((tq, kc), NEG, out_ref.dtype)
        return carry
    lax.fori_loop(n, nc, fill, 0)


def _dsa_select(qi, tail, tail_bf, kk, tq=128, kc=512):
    S = qi.shape[0]
    kc = min(kc, S)
    nc = S // kc
    return pl.pallas_call(
        functools.partial(_dsa_select_kernel, tq=tq, kc=kc, nc=nc, kk=kk),
        grid=(S // tq,),
        in_specs=[pl.BlockSpec((tq, qi.shape[1]), lambda i: (i, 0)),
                  pl.BlockSpec((tq, LANES), lambda i: (i, 0)),
                  pl.BlockSpec((S, LANES), lambda i: (0, 0))],
        out_specs=pl.BlockSpec((nc, tq, kc), lambda i: (0, i, 0)),
        out_shape=jax.ShapeDtypeStruct((nc, S, kc), BF16),
        scratch_shapes=[pltpu.VMEM((nc, tq, kc), I32),
                        pltpu.VMEM((N_IDX, tq, LANES), BF16),
                        pltpu.VMEM((N_IDX, tq, LANES), F32)],
        compiler_params=_cparams(("parallel",)), name="dsa_select")(qi, tail, tail_bf)


def _select_rows_kernel(sc_ref, lim_ref, out_ref, key_sc, *, R, kc, nc, kk):
    lim = lim_ref[...]
    lane = lax.broadcasted_iota(I32, (R, kc), 1)

    def prep(c, carry):
        key_sc[c] = _float_key(jnp.where(c * kc + lane <= lim, sc_ref[c], -jnp.inf))
        return carry
    lax.fori_loop(0, nc, prep, 0)
    _topk_bias(key_sc, nc, out_ref, lim, kk=kk, R=R, kc=kc)


def _select_rows(scores_cm, lim, kk):
    nc, R, kc = scores_cm.shape
    return pl.pallas_call(
        functools.partial(_select_rows_kernel, R=R, kc=kc, nc=nc, kk=kk),
        grid=(1,),
        in_specs=[pl.BlockSpec((nc, R, kc), lambda i: (0, 0, 0)),
                  pl.BlockSpec((R, 1), lambda i: (0, 0))],
        out_specs=pl.BlockSpec((nc, R, kc), lambda i: (0, 0, 0)),
        out_shape=jax.ShapeDtypeStruct((nc, R, kc), BF16),
        scratch_shapes=[pltpu.VMEM((nc, R, kc), I32)],
        compiler_params=_cparams(("arbitrary",)), name="select_rows")(scores_cm, lim)


def _online_softmax(s, m_prev, l_prev, reps):
    m_new = jnp.maximum(m_prev, jnp.max(s, axis=1, keepdims=True))
    alpha = jnp.exp2(m_prev - m_new)
    p = jnp.exp2(s - _tile_lanes(m_new, reps))
    l_new = alpha * l_prev + jnp.sum(p, axis=1, keepdims=True)
    return p, alpha, m_new, l_new


def _moba_top_blocks(gate, n_past):
    lane = lax.broadcasted_iota(I32, gate.shape, 1)
    g = jnp.where(lane < n_past, gate, -jnp.inf)
    ind = jnp.zeros(gate.shape, F32)
    for _ in range(MOBA_TOPK):
        mx = jnp.max(g, axis=1, keepdims=True)
        am = jnp.min(jnp.where(g == mx, lane, 2 * LANES), axis=1, keepdims=True)
        hit = jnp.where(mx > -jnp.inf, jnp.where(lane == am, 1.0, 0.0), 0.0)
        ind = ind + hit
        g = jnp.where(lane == am, -jnp.inf, g)
    return ind


def _block_expand(j, kc):
    r_i = lax.broadcasted_iota(I32, (LANES, kc), 0)
    c_i = lax.broadcasted_iota(I32, (LANES, kc), 1)
    return jnp.where(r_i == (j * kc + c_i) // MOBA_BLOCK, 1.0, 0.0).astype(BF16)


def _flash_kernel(*refs, mode, tq, kc, nj, npairs):
    i = pl.program_id(0)
    j = pl.program_id(1)
    q_ref, k_ref, v_ref = refs[0], refs[1], refs[2]
    pos = 3
    if mode == "bias":
        b_ref = refs[pos]; pos += 1
    elif mode == "moba":
        km_ref = refs[pos]; pos += 1
    elif mode == "diff":
        lam_ref = refs[pos]; pos += 1
    o_ref = refs[pos]; pos += 1
    m_sc, l_sc, acc_sc = refs[pos], refs[pos + 1], refs[pos + 2]
    if mode == "moba":
        sel_sc = refs[pos + 3]
    jmax = ((i + 1) * tq - 1) // kc
    jdiag = (i * tq) // kc
    reps = kc // LANES
    nblk = kc // MOBA_BLOCK
    lane = lax.broadcasted_iota(I32, (tq, LANES), 1)
    lo = lane < 64

    def q_maps(p, scale):
        q2 = q_ref[:, p * LANES:(p + 1) * LANES]
        if scale != 1.0:
            q2 = q2 * scale
        return (jnp.where(lo, q2, 0.0).astype(BF16), jnp.where(lo, 0.0, q2).astype(BF16))

    @pl.when(j == 0)
    def _():
        m_sc[...] = jnp.full(m_sc.shape, NEG, F32)
        l_sc[...] = jnp.zeros(l_sc.shape, F32)
        acc_sc[...] = jnp.zeros(acc_sc.shape, F32)
        if mode == "moba":
            n_past = (i * tq + lax.broadcasted_iota(I32, (tq, 1), 0)) // MOBA_BLOCK
            for p in range(npairs):
                km2 = km_ref[:, p * LANES:(p + 1) * LANES].astype(BF16)
                for e, qm in enumerate(q_maps(p, 1.0)):
                    sel_sc[2 * p + e] = _moba_top_blocks(_dot_nt(qm, km2), n_past)

    def step(diag):
        if mode == "moba":
            row_t = i * tq + lax.broadcasted_iota(I32, (tq, MOBA_BLOCK), 0)
            lane_b = lax.broadcasted_iota(I32, (tq, MOBA_BLOCK), 1)
        if mode == "bias":
            bias = b_ref[0].astype(F32)
        elif diag and mode == "diff":
            causal = (j * kc + lax.broadcasted_iota(I32, (tq, kc), 1)
                      <= i * tq + lax.broadcasted_iota(I32, (tq, kc), 0))
        for p in range(npairs):
            k2 = k_ref[:, p * LANES:(p + 1) * LANES]
            v2 = v_ref[:, p * LANES:(p + 1) * LANES]
            pv, al = [], []
            for e, qm in enumerate(q_maps(p, Q_SCALE)):
                h = 2 * p + e
                s = _dot_nt(qm, k2)
                if mode == "bias":
                    s = s + bias
                elif mode == "moba":
                    sel = sel_sc[h]
                    parts = []
                    for c in range(nblk):
                        blk = j * nblk + c
                        sb = s[:, c * MOBA_BLOCK:(c + 1) * MOBA_BLOCK]
                        picked = jnp.sum(jnp.where(lane == blk, sel, 0.0), axis=1, keepdims=True)
                        sc = sb + jnp.where(picked > 0.5, 0.0, NEG)
                        if diag:
                            own = jnp.where(blk * MOBA_BLOCK + lane_b <= row_t, sb, NEG)
                            sc = jnp.where(row_t // MOBA_BLOCK == blk, own, sc)
                        parts.append(sc)
                    s = jnp.concatenate(parts, axis=1) if nblk > 1 else parts[0]
                elif diag:
                    s = jnp.where(causal, s, NEG)
                pr, alpha, m_new, l_new = _online_softmax(s, m_sc[h], l_sc[h], reps)
                m_sc[h] = m_new
                l_sc[h] = l_new
                pv.append(_dot(pr.astype(BF16), v2))
                al.append(alpha)
            if mode == "diff":
                acc_sc[2 * p] = acc_sc[2 * p] * al[0] + pv[0]
                acc_sc[2 * p + 1] = acc_sc[2 * p + 1] * al[1] + pv[1]
            else:
                acc_sc[p] = acc_sc[p] * jnp.where(lo, al[0], al[1]) + jnp.where(lo, pv[0], pv[1])

    if mode == "bias":
        pl.when(j <= jmax)(lambda: step(False))
    else:
        pl.when(j < jdiag)(lambda: step(False))
        pl.when(jnp.logical_and(j >= jdiag, j <= jmax))(lambda: step(True))

    @pl.when(j == nj - 1)
    def _():
        for p in range(npairs):
            if mode == "diff":
                o = acc_sc[2 * p] / l_sc[2 * p] - lam_ref[0, 0] * (acc_sc[2 * p + 1] / l_sc[2 * p + 1])
            else:
                o = acc_sc[p] / jnp.where(lo, l_sc[2 * p], l_sc[2 * p + 1])
            o_ref[:, p * LANES:(p + 1) * LANES] = o


def _flash_tiles(mode, S):
    tq, kc = {"bias": (512, 512), "moba": (512, 1024), "diff": (512, 1024)}[mode]
    return min(tq, S), min(kc, S)


def _flash(q, k_bf, v_bf, mode, bias=None, kmean=None, lam=None):
    S, W = q.shape
    tq, kc = _flash_tiles(mode, S)
    assert kc % MOBA_BLOCK == 0 and tq % MOBA_BLOCK == 0
    ni, nj = S // tq, S // kc
    npairs = W // LANES
    jm = lambda i, j: jnp.minimum(j, ((i + 1) * tq - 1) // kc)
    in_specs = [pl.BlockSpec((tq, W), lambda i, j: (i, 0)),
                pl.BlockSpec((kc, W), lambda i, j: (jm(i, j), 0)),
                pl.BlockSpec((kc, W), lambda i, j: (jm(i, j), 0))]
    args = [q, k_bf, v_bf]
    nmaps = 2 * npairs
    scratch = [pltpu.VMEM((nmaps, tq, LANES), F32), pltpu.VMEM((nmaps, tq, LANES), F32),
               pltpu.VMEM((nmaps if mode == "diff" else npairs, tq, LANES), F32)]
    if mode == "bias":
        in_specs.append(pl.BlockSpec((1, tq, kc), lambda i, j: (jm(i, j), i, 0)))
        args.append(bias)
    elif mode == "moba":
        in_specs.append(pl.BlockSpec(kmean.shape, lambda i, j: (0, 0)))
        args.append(kmean)
        scratch.append(pltpu.VMEM((nmaps, tq, LANES), F32))
    elif mode == "diff":
        in_specs.append(pl.BlockSpec(memory_space=pltpu.SMEM))
        args.append(lam.reshape(1, 1))
    return pl.pallas_call(
        functools.partial(_flash_kernel, mode=mode, tq=tq, kc=kc, nj=nj, npairs=npairs),
        grid=(ni, nj), in_specs=in_specs,
        out_specs=pl.BlockSpec((tq, W), lambda i, j: (i, 0)),
        out_shape=jax.ShapeDtypeStruct((S, W), F32), scratch_shapes=scratch,
        compiler_params=_cparams(("parallel", "arbitrary"),
                                 VMEM_LIMIT_LARGE if mode == "diff" else VMEM_LIMIT),
        name="flash_" + mode)(*args)


def _block_mean_kernel(k_ref, o_ref, *, nb, per):
    i = pl.program_id(0)

    @pl.when(i < nb // per)
    def _():
        x = k_ref[...]
        o_ref[...] = jnp.mean(x.reshape(per, MOBA_BLOCK, x.shape[1]), axis=1)

    @pl.when(i >= nb // per)
    def _():
        o_ref[...] = jnp.zeros(o_ref.shape, F32)


def _block_means(kb):
    S, W = kb.shape
    nb, per = S // MOBA_BLOCK, 8
    last = nb // per - 1
    return pl.pallas_call(
        functools.partial(_block_mean_kernel, nb=nb, per=per),
        grid=(LANES // per,),
        in_specs=[pl.BlockSpec((per * MOBA_BLOCK, W), lambda i: (jnp.minimum(i, last), 0))],
        out_specs=pl.BlockSpec((per, W), lambda i: (i, 0)),
        out_shape=jax.ShapeDtypeStruct((LANES, W), F32),
        compiler_params=_cparams(("arbitrary",)), name="block_means")(kb)


def _page_specs(shape, ppc, nj):
    zeros = (0,) * len(shape)

    def make(p):
        return pl.BlockSpec((1,) + tuple(shape),
                            lambda b, j, pt: (pt[b, jnp.minimum(j, nj - 1) * ppc + p],) + zeros)
    return [make(p) for p in range(ppc)]


def _pool_t(cache):
    NP = cache.shape[0]
    nd = cache.ndim
    return jnp.transpose(cache, (0,) + tuple(range(2, nd)) + (1,)).reshape(NP, -1, PAGE)


DEC_CHUNK_PAGES = 4


def _decode_ppc(n_pages, want):
    ppc = min(want, n_pages)
    assert ppc % DEC_CHUNK_PAGES == 0 and n_pages % ppc == 0
    return ppc


def _idx_scores_kernel(pt_ref, q_ref, w_ref, new_ref, *refs, ppc, nj, T):
    j = pl.program_id(1)
    pages, o_ref = refs[:ppc], refs[ppc]
    cpp = DEC_CHUNK_PAGES
    kc = cpp * PAGE
    q = q_ref[0].astype(BF16)
    w = w_ref[0]

    def scores(kt):
        d = jnp.maximum(_dot(q, kt), 0.0) * _tile_lanes(w, kt.shape[1] // LANES)
        return jnp.sum(d.reshape(T, N_IDX, kt.shape[1]), axis=1)

    @pl.when(j < nj)
    def _():
        for c in range(ppc // cpp):
            kt = jnp.concatenate([r[0] for r in pages[c * cpp:(c + 1) * cpp]], axis=1).astype(BF16)
            o_ref[c, 0] = scores(kt)

    @pl.when(j == nj)
    def _():
        sc = scores(new_ref[0].astype(BF16))
        o_ref[0, 0] = jnp.concatenate([sc, jnp.zeros((T, kc - PAGE), F32)], axis=1)
        for c in range(1, ppc // cpp):
            o_ref[c, 0] = jnp.zeros((T, kc), F32)


def _idx_scores(pt, q_rows, w_rows, new_page, pool, ppc):
    B = q_rows.shape[0]
    T = q_rows.shape[1] // N_IDX
    nj = pt.shape[1] // ppc
    cps = ppc // DEC_CHUNK_PAGES
    kc = DEC_CHUNK_PAGES * PAGE
    grid_spec = pltpu.PrefetchScalarGridSpec(
        num_scalar_prefetch=1, grid=(B, nj + 1),
        in_specs=[pl.BlockSpec((1, T * N_IDX, HEAD_DIM), lambda b, j, pt: (b, 0, 0)),
                  pl.BlockSpec((1, T * N_IDX, LANES), lambda b, j, pt: (b, 0, 0)),
                  pl.BlockSpec((1, HEAD_DIM, PAGE), lambda b, j, pt: (b, 0, 0))]
        + _page_specs(pool.shape[1:], ppc, nj),
        out_specs=pl.BlockSpec((cps, 1, T, kc), lambda b, j, pt: (j, b, 0, 0)))
    return pl.pallas_call(
        functools.partial(_idx_scores_kernel, ppc=ppc, nj=nj, T=T),
        grid_spec=grid_spec, out_shape=jax.ShapeDtypeStruct(((nj + 1) * cps, B, T, kc), F32),
        compiler_params=_cparams(("parallel", "arbitrary")), name="idx_scores",
    )(pt, q_rows, w_rows, new_page, *([pool] * ppc))


def _pool_block_mean_kernel(pt_ref, *refs, npg):
    j = pl.program_id(1)
    pages, o_ref = refs[:npg], refs[npg]
    ppb = MOBA_BLOCK // PAGE
    lane = lax.broadcasted_iota(I32, o_ref.shape[1:], 1)

    @pl.when(j == 0)
    def _():
        o_ref[...] = jnp.zeros(o_ref.shape, F32)

    acc = o_ref[0]
    for blk in range(npg // ppb):
        tot = pages[blk * ppb][0]
        for a in range(1, ppb):
            tot = tot + pages[blk * ppb + a][0]
        mean = jnp.sum(tot, axis=1, keepdims=True) * (1.0 / MOBA_BLOCK)
        acc = jnp.where(lane == j * (npg // ppb) + blk, mean, acc)
    o_ref[0] = acc


def _pool_block_means(pt, pool_t, bpc=8):
    B, n_pages = pt.shape
    W = pool_t.shape[1]
    ppb = MOBA_BLOCK // PAGE
    nb = n_pages // ppb
    assert nb <= LANES
    bpc = min(bpc, nb)
    npg = bpc * ppb
    nj = nb // bpc
    grid_spec = pltpu.PrefetchScalarGridSpec(
        num_scalar_prefetch=1, grid=(B, nj),
        in_specs=_page_specs((W, PAGE), npg, nj),
        out_specs=pl.BlockSpec((1, W, LANES), lambda b, j, pt: (b, 0, 0)))
    return pl.pallas_call(
        functools.partial(_pool_block_mean_kernel, npg=npg),
        grid_spec=grid_spec, out_shape=jax.ShapeDtypeStruct((B, W, LANES), F32),
        compiler_params=_cparams(("parallel", "arbitrary")), name="pool_block_means",
    )(pt, *([pool_t] * npg))


def _paged_attn_kernel(pt_ref, *refs, mode, ppc, nj, T, G, W, Wv, vt):
    j = pl.program_id(1)
    q_ref, knew_ref, vnew_ref = refs[0], refs[1], refs[2]
    pos = 3
    if mode == "bias":
        b_ref = refs[pos]; pos += 1
    elif mode == "moba":
        km_ref = refs[pos]; pos += 1
    kpages = refs[pos:pos + ppc]; pos += ppc
    vpages = refs[pos:pos + ppc]; pos += ppc
    o_ref = refs[pos]; pos += 1
    m_sc, l_sc, acc_sc = refs[pos], refs[pos + 1], refs[pos + 2]
    if mode == "moba":
        sel_sc = refs[pos + 3]
    R = T * G
    kc = ppc * PAGE
    row = lax.broadcasted_iota(I32, (R, W), 0)
    col = lax.broadcasted_iota(I32, (R, W), 1)
    q = q_ref[0]
    qrep = jnp.concatenate([jnp.broadcast_to(q[t:t + 1], (G, W)) for t in range(T)], axis=0)
    qraw = jnp.where(col // HEAD_DIM == row % G, qrep, 0.0)
    qbd = (qraw * Q_SCALE).astype(BF16)

    @pl.when(j == 0)
    def _():
        m_sc[...] = jnp.full(m_sc.shape, NEG, F32)
        l_sc[...] = jnp.zeros(l_sc.shape, F32)
        acc_sc[...] = jnp.zeros(acc_sc.shape, F32)
        if mode == "moba":
            n_full = (nj * kc) // MOBA_BLOCK
            gate = _dot(qraw.astype(BF16), km_ref[0].astype(BF16))
            sel_sc[...] = _moba_top_blocks(gate, n_full).astype(BF16)

    def update(s, v):
        pr, alpha, m_new, l_new = _online_softmax(s, m_sc[...], l_sc[...], s.shape[1] // LANES)
        m_sc[...] = m_new
        l_sc[...] = l_new
        pv = _dot_nt(pr.astype(BF16), v) if vt else _dot(pr.astype(BF16), v)
        acc_sc[...] = acc_sc[...] * _tile_lanes(alpha, Wv // LANES) + pv

    @pl.when(j < nj)
    def _():
        k = jnp.concatenate([r[0] for r in kpages], axis=1).astype(BF16)
        v = jnp.concatenate([r[0] for r in vpages], axis=1 if vt else 0).astype(BF16)
        s = _dot(qbd, k)
        if mode == "bias":
            b = jnp.concatenate([b_ref[c, 0] for c in range(ppc // DEC_CHUNK_PAGES)], axis=1).astype(F32)
            s = s + jnp.concatenate([jnp.broadcast_to(b[t:t + 1], (G, kc)) for t in range(T)], axis=0)
        elif mode == "moba":
            s = jnp.where(_dot(sel_sc[...], _block_expand(j, kc)) > 0.5, s, NEG)
        update(s, v)

    @pl.when(j == nj)
    def _():
        s = _dot(qbd, knew_ref[0].astype(BF16))
        if mode == "bias":
            b = b_ref[0, 0][:, :PAGE].astype(F32)
            s = s + jnp.concatenate([jnp.broadcast_to(b[t:t + 1], (G, PAGE)) for t in range(T)], axis=0)
        else:
            r_i = lax.broadcasted_iota(I32, (R, PAGE), 0)
            c_i = lax.broadcasted_iota(I32, (R, PAGE), 1)
            s = jnp.where(c_i <= r_i // G, s, NEG)
        update(s, vnew_ref[0].astype(BF16))
        o = acc_sc[...] / _tile_lanes(l_sc[...], Wv // LANES)
        rv = lax.broadcasted_iota(I32, (R, Wv), 0)
        cv = lax.broadcasted_iota(I32, (R, Wv), 1)
        o = jnp.where(cv // HEAD_DIM == rv % G, o, 0.0)
        o_ref[0] = jnp.sum(o.reshape(T, G, Wv), axis=1)


def _paged_diff_kernel(pt_ref, q_ref, knew_ref, vnew_ref, lam_ref, *refs, ppc, nj, T, H, W):
    j = pl.program_id(1)
    kpages, vpages = refs[:ppc], refs[ppc:2 * ppc]
    o_ref, m_sc, l_sc, acc_sc = refs[2 * ppc:2 * ppc + 4]
    G = 2 * H
    R = G * T
    RH = 2 * T
    row = lax.broadcasted_iota(I32, (R, W), 0)
    col = lax.broadcasted_iota(I32, (R, W), 1)
    qrep = jnp.concatenate([q_ref[0]] * G, axis=0)
    qbd = (jnp.where(col // HEAD_DIM == row // T, qrep, 0.0) * Q_SCALE).astype(BF16)

    @pl.when(j == 0)
    def _():
        m_sc[...] = jnp.full(m_sc.shape, NEG, F32)
        l_sc[...] = jnp.zeros(l_sc.shape, F32)
        acc_sc[...] = jnp.zeros(acc_sc.shape, F32)

    def update(s, vrefs):
        pr, alpha, m_new, l_new = _online_softmax(s, m_sc[...], l_sc[...], s.shape[1] // LANES)
        m_sc[...] = m_new
        l_sc[...] = l_new
        prb = pr.astype(BF16)
        for h in range(H):
            vh = jnp.concatenate([r[0, pl.ds(h, PAGE, stride=H), :] for r in vrefs],
                                 axis=0).astype(BF16)
            rows = slice(h * RH, (h + 1) * RH)
            acc_sc[rows, :] = acc_sc[rows, :] * alpha[rows, :] + _dot(prb[rows, :], vh)

    @pl.when(j < nj)
    def _():
        k = jnp.concatenate([r[0] for r in kpages], axis=1).astype(BF16)
        update(_dot(qbd, k), vpages)

    @pl.when(j == nj)
    def _():
        s = _dot(qbd, knew_ref[0].astype(BF16))
        r_i = lax.broadcasted_iota(I32, (R, PAGE), 0)
        c_i = lax.broadcasted_iota(I32, (R, PAGE), 1)
        update(jnp.where(c_i <= r_i % T, s, NEG), [vnew_ref])
        o = acc_sc[...] / l_sc[...]
        outs = []
        for h in range(H):
            outs.append(o[h * RH:h * RH + T, :] - lam_ref[0, 0] * o[h * RH + T:(h + 1) * RH, :])
        o_ref[0] = jnp.concatenate(outs, axis=1)


def _paged_diff(pt, q3, knew, vnew, pool_k, pool_v, lam, ppc):
    B, T, W = q3.shape
    E = pool_v.shape[2]
    H = pool_v.shape[1] // PAGE
    assert E == LANES and W == 2 * H * HEAD_DIM and (2 * T) % 8 == 0
    nj = pt.shape[1] // ppc
    cmap3 = lambda b, j, pt: (b, 0, 0)
    R = 2 * H * T
    grid_spec = pltpu.PrefetchScalarGridSpec(
        num_scalar_prefetch=1, grid=(B, nj + 1),
        in_specs=[pl.BlockSpec((1, T, W), cmap3), pl.BlockSpec((1, W, PAGE), cmap3),
                  pl.BlockSpec((1, PAGE * H, E), cmap3), pl.BlockSpec(memory_space=pltpu.SMEM)]
        + _page_specs(pool_k.shape[1:], ppc, nj) + _page_specs(pool_v.shape[1:], ppc, nj),
        out_specs=pl.BlockSpec((1, T, H * E), cmap3),
        scratch_shapes=[pltpu.VMEM((R, LANES), F32), pltpu.VMEM((R, LANES), F32), pltpu.VMEM((R, E), F32)])
    return pl.pallas_call(
        functools.partial(_paged_diff_kernel, ppc=ppc, nj=nj, T=T, H=H, W=W),
        grid_spec=grid_spec, out_shape=jax.ShapeDtypeStruct((B, T, H * E), F32),
        compiler_params=_cparams(("parallel", "arbitrary")), name="paged_diff",
    )(pt, q3, knew, vnew, lam.reshape(1, 1), *([pool_k] * ppc), *([pool_v] * ppc))


def _paged_attn(pt, q3, knew, vnew, pool_k, pool_v, mode, vt, ppc, bias=None, kmean=None):
    B, T, W = q3.shape
    Wv = pool_v.shape[1] if vt else pool_v.shape[2]
    G = W // HEAD_DIM
    nj = pt.shape[1] // ppc
    cps = ppc // DEC_CHUNK_PAGES
    cmap = lambda b, j, pt: (b, 0, 0)
    in_specs = [pl.BlockSpec((1, T, W), cmap), pl.BlockSpec((1,) + knew.shape[1:], cmap),
                pl.BlockSpec((1,) + vnew.shape[1:], cmap)]
    args = [q3, knew, vnew]
    scratch = [pltpu.VMEM((T * G, LANES), F32), pltpu.VMEM((T * G, LANES), F32),
               pltpu.VMEM((T * G, Wv), F32)]
    if mode == "bias":
        assert bias.shape[0] == (nj + 1) * cps
        in_specs.append(pl.BlockSpec((cps, 1, T, bias.shape[3]), lambda b, j, pt: (j, b, 0, 0)))
        args.append(bias)
    elif mode == "moba":
        in_specs.append(pl.BlockSpec((1, W, LANES), cmap))
        args.append(kmean)
        scratch.append(pltpu.VMEM((T * G, LANES), BF16))
    in_specs += _page_specs(pool_k.shape[1:], ppc, nj) + _page_specs(pool_v.shape[1:], ppc, nj)
    args += [pool_k] * ppc + [pool_v] * ppc
    grid_spec = pltpu.PrefetchScalarGridSpec(
        num_scalar_prefetch=1, grid=(B, nj + 1), in_specs=in_specs,
        out_specs=pl.BlockSpec((1, T, Wv), cmap), scratch_shapes=scratch)
    return pl.pallas_call(
        functools.partial(_paged_attn_kernel, mode=mode, ppc=ppc, nj=nj, T=T, G=G, W=W, Wv=Wv, vt=vt),
        grid_spec=grid_spec, out_shape=jax.ShapeDtypeStruct((B, T, Wv), F32),
        compiler_params=_cparams(("parallel", "arbitrary")), name="paged_" + mode)(pt, *args)


def _lambda_kernel(a_ref, o_ref, *, lam_init):
    a = a_ref[...]
    d1 = jnp.sum(a[0:1] * a[1:2], axis=1, keepdims=True)
    d2 = jnp.sum(a[2:3] * a[3:4], axis=1, keepdims=True)
    o_ref[...] = jnp.exp(d1) - jnp.exp(d2) + lam_init


def _diff_lambda(lq1, lk1, lq2, lk2, lam_init):
    a = jnp.zeros((8, LANES), F32).at[:4, :HEAD_DIM].set(jnp.stack([lq1, lk1, lq2, lk2]))
    return pl.pallas_call(functools.partial(_lambda_kernel, lam_init=lam_init),
                          out_shape=jax.ShapeDtypeStruct((1, 1), F32), name="diff_lambda")(a)


def _pad_new_page(x3):
    B, T, W = x3.shape
    return jnp.pad(x3, ((0, 0), (0, PAGE - T), (0, 0)))


def _new_page_t(x3):
    B, T, W = x3.shape
    return jnp.pad(jnp.swapaxes(x3, 1, 2), ((0, 0), (0, 0), (0, PAGE - T)))


def kernel(x_prompt, x_sample, cache_a_k, cache_a_v, cache_idx_k, cache_b_k, cache_b_v, cache_c_k, cache_c_v, cache_mem_k, cache_mem_v, page_table, mem_prompt, g_mix, g_mem, g_final, w_in_even, w_out_even, w_in_odd, w_out_odd, lam_q1, lam_k1, lam_q2, lam_k2, g_subln, w_mq, w_mk, w_mv, w_mo):
    Bp, S, D = x_prompt.shape
    Bs, T, _ = x_sample.shape
    assert Bp == 1
    depth = g_mix.shape[0]
    n_pages = page_table.shape[1]
    past = n_pages * PAGE
    assert past % MOBA_BLOCK == 0 and S % MOBA_BLOCK == 0
    NP = cache_a_k.shape[1]
    HA = cache_a_k.shape[3]
    HB = cache_b_k.shape[3]
    HC = cache_c_k.shape[3]
    HM, HDM = cache_mem_k.shape[3], cache_mem_k.shape[4]
    NM = cache_mem_k.shape[2]
    WA, WB, WC, WM = HA * HEAD_DIM, HB * HEAD_DIM, HC * 2 * HEAD_DIM, HM * HDM
    WIQ = N_IDX * HEAD_DIM
    Ms = Bs * T

    cos_p, sin_p = _rope_tables(jnp.arange(S))
    cos_s, sin_s = _rope_tables(jnp.tile(past + jnp.arange(T), Bs))
    xp = x_prompt.reshape(S, D)
    xs = x_sample.reshape(Ms, D)
    mem2 = mem_prompt.reshape(NM, D)
    pt = page_table.astype(I32)

    n_tail = 4 * WA + 4 * WB + WIQ
    even_groups = []
    for gi in range(8):
        kind = "rope" if gi in (0, 1, 4, 5) else "plain"
        even_groups.append((gi * WA, WA, kind, 1.0, gi in (1, 2, 5, 6)))
    even_groups.append((8 * WA, WIQ, "rope", 1.0, False))
    even_groups.append((n_tail, LANES, "tail", float(WIQ) ** -0.5, True))
    odd_groups = [(0, WC, "rope", 1.0, False), (WC, WC, "rope", 1.0, True),
                  (2 * WC, WC, "plain", 1.0, True), (3 * WC, WC, "plain", 1.0, False)]
    mem_groups = [(0, WM, "plain", 1.0, False), (WM, WM, "plain", 1.0, False)]

    outs = {}
    ones = jnp.ones((D,), F32)
    for l in range(depth):
        i = l // 2
        if l % 2 == 0:
            w_in = w_in_even[i]
            w_in = jnp.pad(w_in, ((0, 0), (0, n_tail + LANES - w_in.shape[1]))).astype(BF16)
            w_out = w_out_even[i].astype(BF16)
            (qa, ka, ka_b, va, va_b, ga, qb, kb, kb_b, vb, vb_b, gb, qi, tail, tail_b) = _rms_proj(
                xp, g_mix[l], w_in, even_groups, cos_p, sin_p)
            kk = min(TOPK_TOK, S // 4)
            bias = _dsa_select(qi, tail, tail_b, kk)
            oa = _flash(qa, ka_b, va_b, "bias", bias=bias)
            ob = _flash(qb, kb_b, vb_b, "moba", kmean=_block_means(kb))
            xp = _gated_out(xp, w_out, [(oa, ga), (ob, gb)])
            outs.setdefault("pa_k", []).append(ka.reshape(1, S, HA, HEAD_DIM))
            outs.setdefault("pa_v", []).append(va.reshape(1, S, HA, HEAD_DIM))
            outs.setdefault("pidx_k", []).append(tail[:, :HEAD_DIM].reshape(1, S, HEAD_DIM))
            outs.setdefault("pb_k", []).append(kb.reshape(1, S, HB, HEAD_DIM))
            outs.setdefault("pb_v", []).append(vb.reshape(1, S, HB, HEAD_DIM))
            (qa, ka, _, va, _, ga, qb, kb, _, vb, _, gb, qi, tail, _) = _rms_proj(
                xs, g_mix[l], w_in, even_groups, cos_s, sin_s)
            q_rows = qi.reshape(Bs, T * N_IDX, HEAD_DIM)
            w_rows = jnp.broadcast_to(tail[:, HEAD_DIM:HEAD_DIM + N_IDX].reshape(Bs, T * N_IDX, 1),
                                      (Bs, T * N_IDX, LANES))
            tail3 = tail.reshape(Bs, T, LANES)
            ki_new = _new_page_t(tail3[:, :, :HEAD_DIM])
            ppc = _decode_ppc(n_pages, 16)
            sc = _idx_scores(pt, q_rows, w_rows, ki_new, _pool_t(cache_idx_k[i]), ppc)
            ncs, _, _, kcs = sc.shape
            lim = (past + jnp.tile(jnp.arange(T, dtype=I32), Bs)).reshape(Ms, 1)
            bias_s = _select_rows(sc.reshape(ncs, Ms, kcs), lim, min(TOPK_TOK, (past + T) // 4))
            oa = _paged_attn(pt, qa.reshape(Bs, T, WA), _new_page_t(ka.reshape(Bs, T, WA)),
                             _new_page_t(va.reshape(Bs, T, WA)),
                             _pool_t(cache_a_k[i]), _pool_t(cache_a_v[i]),
                             "bias", True, ppc, bias=bias_s.reshape(ncs, Bs, T, kcs))
            pool_bk = _pool_t(cache_b_k[i])
            ob = _paged_attn(pt, qb.reshape(Bs, T, WB), _new_page_t(kb.reshape(Bs, T, WB)),
                             _new_page_t(vb.reshape(Bs, T, WB)),
                             pool_bk, _pool_t(cache_b_v[i]),
                             "moba", True, ppc, kmean=_pool_block_means(pt, pool_bk, 16))
            xs = _gated_out(xs, w_out, [(oa.reshape(Ms, WA), ga), (ob.reshape(Ms, WB), gb)])
            outs.setdefault("sa_k", []).append(ka.reshape(Bs, T, HA, HEAD_DIM))
            outs.setdefault("sa_v", []).append(va.reshape(Bs, T, HA, HEAD_DIM))
            outs.setdefault("sidx_k", []).append(tail[:, :HEAD_DIM].reshape(Bs, T, HEAD_DIM))
            outs.setdefault("sb_k", []).append(kb.reshape(Bs, T, HB, HEAD_DIM))
            outs.setdefault("sb_v", []).append(vb.reshape(Bs, T, HB, HEAD_DIM))
        else:
            lam_init = 0.8 - 0.6 * math.exp(-0.3 * l)
            lam = _diff_lambda(lam_q1[i], lam_k1[i], lam_q2[i], lam_k2[i], lam_init)
            w_in = w_in_odd[i].astype(BF16)
            w_out = w_out_odd[i].astype(BF16)
            q, k, k_b, v, v_b, g = _rms_proj(xp, g_mix[l], w_in, odd_groups, cos_p, sin_p)
            o = _flash(q, k_b, v_b, "diff", lam=lam)
            xp = _gated_out(xp, w_out, [(o, g)], gsub=g_subln[i], post_scale=1.0 - lam_init)
            outs.setdefault("pc_k", []).append(k.reshape(1, S, HC, 2, HEAD_DIM))
            outs.setdefault("pc_v", []).append(v.reshape(1, S, HC, 2 * HEAD_DIM))
            q, k, _, v, _, g = _rms_proj(xs, g_mix[l], w_in, odd_groups, cos_s, sin_s)
            o = _paged_diff(pt, q.reshape(Bs, T, WC), _new_page_t(k.reshape(Bs, T, WC)),
                            _pad_new_page(v.reshape(Bs, T, WC)).reshape(Bs, PAGE * HC, 2 * HEAD_DIM),
                            _pool_t(cache_c_k[i]), cache_c_v[i].reshape(NP, PAGE * HC, 2 * HEAD_DIM), lam,
                            _decode_ppc(n_pages, 8))
            xs = _gated_out(xs, w_out, [(o.reshape(Ms, WC), g)], gsub=g_subln[i], post_scale=1.0 - lam_init)
            outs.setdefault("sc_k", []).append(k.reshape(Bs, T, HC, 2, HEAD_DIM))
            outs.setdefault("sc_v", []).append(v.reshape(Bs, T, HC, 2 * HEAD_DIM))
        w_kv = jnp.concatenate([w_mk[l], w_mv[l]], axis=1).astype(BF16)
        mk, mv = _rms_proj(mem2, ones, w_kv, mem_groups, norm=False)
        w_q = w_mq[l].astype(BF16)
        w_o = w_mo[l].astype(BF16)
        last = l == depth - 1
        q, g = _rms_proj(xp, g_mem[l], w_q, mem_groups)
        o = _mem_attn(q.reshape(1, S, WM), mk.reshape(1, NM, WM), mv.reshape(1, NM, WM), HM, HDM)
        res = _gated_out(xp, w_o, [(o.reshape(S, WM), g)], gfin=g_final if last else None)
        xp, yp = res if last else (res, None)
        q, g = _rms_proj(xs, g_mem[l], w_q, mem_groups)
        o = _mem_attn(q.reshape(Bs, T, WM), cache_mem_k[l].reshape(Bs, NM, WM),
                      cache_mem_v[l].reshape(Bs, NM, WM), HM, HDM)
        res = _gated_out(xs, w_o, [(o.reshape(Ms, WM), g)], gfin=g_final if last else None)
        xs, ys = res if last else (res, None)
        outs.setdefault("pm_k", []).append(mk.reshape(1, NM, HM, HDM))
        outs.setdefault("pm_v", []).append(mv.reshape(1, NM, HM, HDM))

    st = lambda name: jnp.stack(outs[name])
    return (yp.reshape(1, S, D), ys.reshape(Bs, T, D),
            st("pa_k"), st("pa_v"), st("pidx_k"), st("pb_k"), st("pb_v"), st("pc_k"), st("pc_v"),
            st("pm_k"), st("pm_v"),
            st("sa_k"), st("sa_v"), st("sidx_k"), st("sb_k"), st("sb_v"), st("sc_k"), st("sc_v"))
```

```python
import functools
import math

import jax
import jax.numpy as jnp
from jax import lax
from jax.experimental import pallas as pl
from jax.experimental.pallas import tpu as pltpu

F32 = jnp.float32
BF16 = jnp.bfloat16
I32 = jnp.int32
I16 = jnp.int16

HEAD_DIM = 64
PAGE = 128
N_IDX = 8
TOPK_TOK = 256
MOBA_BLOCK = 256
MOBA_TOPK = 3
ROPE_THETA = 10000.0
EPS = 1e-6
LANES = 128
NEG = -1e30
Q_SCALE = HEAD_DIM ** -0.5 * math.log2(math.e)
INT_MIN = -2147483648
KEY_NEG_INF = -2139095041
VMEM_LIMIT = 56 * 1024 * 1024
VMEM_LIMIT_LARGE = 60 * 1024 * 1024


def _cparams(sem, vmem=VMEM_LIMIT):
    return pltpu.CompilerParams(dimension_semantics=sem, vmem_limit_bytes=vmem)


def _dot_nt(a, b):
    return lax.dot_general(a, b, (((1,), (1,)), ((), ())), preferred_element_type=F32)


def _dot(a, b):
    return jnp.dot(a, b, preferred_element_type=F32)


def _tile_lanes(x, reps):
    return x if reps == 1 else jnp.concatenate([x] * reps, axis=1)


def _rope_tables(pos):
    inv = ROPE_THETA ** (-jnp.arange(0, HEAD_DIM, 2, dtype=F32) / HEAD_DIM)
    ang = pos.astype(F32)[:, None] * inv[None, :]
    cos, sin = jnp.cos(ang), jnp.sin(ang)
    cos128 = jnp.concatenate([cos] * 4, axis=1)
    sin128 = jnp.concatenate([-sin, sin, -sin, sin], axis=1)
    return cos128, sin128


def _rope128(z, c, s):
    lane = lax.broadcasted_iota(I32, z.shape, 1)
    first = (lane & 63) < 32
    partner = jnp.where(first, pltpu.roll(z, 96, axis=1), pltpu.roll(z, 32, axis=1))
    return z * c + partner * s


def _proj_kernel(*refs, groups, norm, has_rope):
    x_ref, g_ref, w_ref = refs[0], refs[1], refs[2]
    pos = 3
    if has_rope:
        c = refs[3][...]
        s = refs[4][...]
        pos = 5
    outs = refs[pos:]
    x = x_ref[...]
    if norm:
        x = x * lax.rsqrt(jnp.mean(x * x, axis=-1, keepdims=True) + EPS) * g_ref[...]
    xb = x.astype(BF16)
    oi = 0
    for start, width, kind, scale, dup in groups:
        z = _dot(xb, w_ref[:, start:start + width])
        if kind == "rope":
            z = jnp.concatenate(
                [_rope128(z[:, a:a + LANES], c, s) for a in range(0, width, LANES)], axis=1)
        elif kind == "tail":
            lane = lax.broadcasted_iota(I32, z.shape, 1)
            z = jnp.where(lane < 64, _rope128(z, c, s),
                          jnp.where(lane < 64 + N_IDX, z * scale, 0.0))
        outs[oi][...] = z
        oi += 1
        if dup:
            outs[oi][...] = z.astype(BF16)
            oi += 1


def _rms_proj(x, g, w, groups, cos=None, sin=None, norm=True):
    M, D = x.shape
    tm = min(256, M)
    has_rope = cos is not None
    in_specs = [pl.BlockSpec((tm, D), lambda i: (i, 0)),
                pl.BlockSpec((1, D), lambda i: (0, 0)),
                pl.BlockSpec(w.shape, lambda i: (0, 0))]
    args = [x, g.reshape(1, D), w]
    if has_rope:
        in_specs += [pl.BlockSpec((tm, LANES), lambda i: (i, 0))] * 2
        args += [cos, sin]
    out_shape, out_specs = [], []
    for _, width, _, _, dup in groups:
        out_shape.append(jax.ShapeDtypeStruct((M, width), F32))
        out_specs.append(pl.BlockSpec((tm, width), lambda i: (i, 0)))
        if dup:
            out_shape.append(jax.ShapeDtypeStruct((M, width), BF16))
            out_specs.append(pl.BlockSpec((tm, width), lambda i: (i, 0)))
    return pl.pallas_call(
        functools.partial(_proj_kernel, groups=tuple(groups), norm=norm, has_rope=has_rope),
        grid=(M // tm,), in_specs=in_specs, out_specs=out_specs, out_shape=out_shape,
        compiler_params=_cparams(("parallel",)), name="rms_proj")(*args)


def _silu(g):
    return g * (1.0 / (1.0 + jnp.exp(-g)))


def _gated_out_kernel(*refs, widths, subnorm, post_scale, final_norm):
    resid_ref, w_ref = refs[0], refs[1]
    pos = 2
    if subnorm:
        gsub = refs[pos][...]
        pos += 1
    if final_norm:
        gfin = refs[pos][...]
        pos += 1
    acc = resid_ref[...]
    off = 0
    for wd in widths:
        o = refs[pos][...]
        g = refs[pos + 1][...]
        pos += 2
        if subnorm:
            parts = []
            for a in range(0, wd, LANES):
                oh = o[:, a:a + LANES]
                parts.append(oh * lax.rsqrt(jnp.mean(oh * oh, axis=-1, keepdims=True) + EPS)
                             * gsub * post_scale)
            o = jnp.concatenate(parts, axis=1)
        z = (o * _silu(g)).astype(BF16)
        acc = acc + _dot(z, w_ref[off:off + wd, :])
        off += wd
    out_ref = refs[pos]
    out_ref[...] = acc
    if final_norm:
        refs[pos + 1][...] = acc * lax.rsqrt(jnp.mean(acc * acc, axis=-1, keepdims=True) + EPS) * gfin


def _gated_out(resid, w, parts, gsub=None, post_scale=1.0, gfin=None):
    M, D = resid.shape
    tm = min(256, M)
    widths = tuple(o.shape[1] for o, _ in parts)
    in_specs = [pl.BlockSpec((tm, D), lambda i: (i, 0)), pl.BlockSpec(w.shape, lambda i: (0, 0))]
    args = [resid, w]
    if gsub is not None:
        in_specs.append(pl.BlockSpec((1, LANES), lambda i: (0, 0)))
        args.append(gsub.reshape(1, LANES))
    if gfin is not None:
        in_specs.append(pl.BlockSpec((1, D), lambda i: (0, 0)))
        args.append(gfin.reshape(1, D))
    for o, g in parts:
        wd = o.shape[1]
        in_specs += [pl.BlockSpec((tm, wd), lambda i: (i, 0))] * 2
        args += [o, g]
    out_shape = [jax.ShapeDtypeStruct((M, D), F32)]
    out_specs = [pl.BlockSpec((tm, D), lambda i: (i, 0))]
    if gfin is not None:
        out_shape.append(jax.ShapeDtypeStruct((M, D), F32))
        out_specs.append(pl.BlockSpec((tm, D), lambda i: (i, 0)))
    res = pl.pallas_call(
        functools.partial(_gated_out_kernel, widths=widths, subnorm=gsub is not None,
                          post_scale=post_scale, final_norm=gfin is not None),
        grid=(M // tm,), in_specs=in_specs, out_specs=out_specs, out_shape=out_shape,
        compiler_params=_cparams(("parallel",)), name="gated_out")(*args)
    return res if gfin is not None else res[0]


def _mem_attn_kernel(q_ref, k_ref, v_ref, o_ref, *, heads, hd):
    scale = hd ** -0.5
    outs = []
    for h in range(heads):
        q = q_ref[0, :, h * hd:(h + 1) * hd].astype(BF16)
        k = k_ref[0, :, h * hd:(h + 1) * hd].astype(BF16)
        v = v_ref[0, :, h * hd:(h + 1) * hd].astype(BF16)
        s = _dot_nt(q, k) * scale
        m = jnp.max(s, axis=1, keepdims=True)
        p = jnp.exp(s - m)
        l = jnp.sum(p, axis=1, keepdims=True)
        outs.append(_dot((p / l).astype(BF16), v))
    o_ref[0] = jnp.concatenate(outs, axis=1)


def _mem_attn(q3, mk3, mv3, heads, hd):
    B, T, W = q3.shape
    Bk, N, _ = mk3.shape
    tq = min(256, T)
    kidx = (lambda b, i: (b, 0, 0)) if Bk == B else (lambda b, i: (0, 0, 0))
    return pl.pallas_call(
        functools.partial(_mem_attn_kernel, heads=heads, hd=hd),
        grid=(B, T // tq),
        in_specs=[pl.BlockSpec((1, tq, W), lambda b, i: (b, i, 0)),
                  pl.BlockSpec((1, N, W), kidx), pl.BlockSpec((1, N, W), kidx)],
        out_specs=pl.BlockSpec((1, tq, W), lambda b, i: (b, i, 0)),
        out_shape=jax.ShapeDtypeStruct((B, T, W), F32),
        compiler_params=_cparams(("parallel", "parallel")), name="mem_attn")(q3, mk3, mv3)


def _float_key(x):
    bits = lax.bitcast_convert_type(x, I32)
    return jnp.where(bits < 0, bits ^ 0x7FFFFFFF, bits)


def _key_hi(key):
    return lax.shift_right_arithmetic(key, 16).astype(I16)


def _topk_bias(key_ref, hi_ref, n, out_ref, lim, *, kk, R, kc):
    slabs = kc // LANES

    def count(thr_b, strict):
        def body(c, acc):
            kt = key_ref[c]
            for a in range(slabs):
                blk = kt[:, a * LANES:(a + 1) * LANES]
                hit = (blk > thr_b) if strict else (blk >= thr_b)
                acc = acc + jnp.where(hit, 1, 0)
            return acc
        acc = lax.fori_loop(0, n, body, jnp.zeros((R, LANES), I32))
        return jnp.sum(acc, axis=1, keepdims=True)

    def count_hi(thr16):
        def body(c, acc):
            kt = hi_ref[c]
            for a in range(slabs):
                acc = acc + jnp.where(kt[:, a * LANES:(a + 1) * LANES] >= thr16, jnp.int16(1), jnp.int16(0))
            return acc
        acc = lax.fori_loop(0, n, body, jnp.zeros((R, LANES), I16))
        return jnp.sum(acc.astype(I32), axis=1, keepdims=True)

    def coarse(b, t_hi):
        cand = t_hi | jnp.left_shift(jnp.int32(1), 15 - b)
        cnt = count_hi((cand - 32768).astype(I16))
        return jnp.where(cnt >= kk, cand, t_hi)

    def search(state):
        b, t_u, done, _ = state
        cand_u = t_u | jnp.left_shift(jnp.int32(1), 31 - b)
        cnt = count(cand_u ^ INT_MIN, False)
        t_new = jnp.where(done > 0, t_u, jnp.where(cnt >= kk, cand_u, t_u))
        done = jnp.where(cnt == kk, 1, done)
        return b + 1, t_new, done, jnp.min(done)

    zeros = jnp.zeros((R, LANES), I32)
    t_hi = lax.fori_loop(0, 16, coarse, zeros)
    _, t_u, _, _ = lax.while_loop(lambda st: jnp.logical_and(st[0] < 32, st[3] == 0), search,
                                  (jnp.int32(16), jnp.left_shift(t_hi, 16), zeros, jnp.int32(0)))
    thr = t_u ^ INT_MIN
    cnt_gt = count(thr, True)
    need = kk - cnt_gt
    n_eq = count(thr, False) - cnt_gt
    surplus = jnp.where(n_eq > need, jnp.where(thr[:, :1] > KEY_NEG_INF, 1, 0), 0)
    has_ties = jnp.max(surplus)
    thr_full = _tile_lanes(thr, slabs)
    lane = lax.broadcasted_iota(I32, (R, kc), 1)

    @pl.when(has_ties == 0)
    def _():
        def body(c, carry):
            sel = jnp.where(key_ref[c] >= thr_full, 0.0, NEG)
            out_ref[c] = jnp.where(c * kc + lane <= lim, sel, NEG).astype(out_ref.dtype)
            return carry
        lax.fori_loop(0, n, body, 0)

    @pl.when(has_ties != 0)
    def _():
        r_i = lax.broadcasted_iota(I32, (kc, kc), 0)
        c_i = lax.broadcasted_iota(I32, (kc, kc), 1)
        upper = jnp.where(r_i <= c_i, 1.0, 0.0).astype(BF16)
        need_f = need.astype(F32)

        def body(c, run):
            kt = key_ref[c]
            eq = jnp.where(kt == thr_full, 1.0, 0.0)
            rank = run + _dot(eq.astype(BF16), upper)
            take_eq = jnp.where(rank <= need_f, eq, 0.0)
            sel = jnp.where(kt > thr_full, 0.0, jnp.where(take_eq > 0.5, 0.0, NEG))
            out_ref[c] = jnp.where(c * kc + lane <= lim, sel, NEG).astype(out_ref.dtype)
            return run + jnp.sum(eq, axis=1, keepdims=True)
        lax.fori_loop(0, n, body, jnp.zeros((R, 1), F32))


def _dsa_select_kernel(qi_ref, wq_ref, kt_ref, out_ref, key_sc, hi_sc, qh_sc, wb_sc, *, tq, kc, nc, kk):
    i = pl.program_id(0)
    t0 = i * tq
    n = (t0 + tq + kc - 1) // kc
    lane = lax.broadcasted_iota(I32, (tq, LANES), 1)
    q = qi_ref[...]
    wq = wq_ref[...]
    for h in range(N_IDX):
        slab = q[:, (h // 2) * LANES:(h // 2 + 1) * LANES]
        if h % 2 == 1:
            slab = pltpu.roll(slab, 64, axis=1)
        qh_sc[h] = jnp.where(lane < 64, slab, 0.0).astype(BF16)
        wcol = jnp.sum(jnp.where(lane == 64 + h, wq, 0.0), axis=1, keepdims=True)
        wb_sc[h] = jnp.broadcast_to(wcol, (tq, LANES))
    row_t = t0 + lax.broadcasted_iota(I32, (tq, 1), 0)
    lane_k = lax.broadcasted_iota(I32, (tq, kc), 1)
    slabs = kc // LANES

    def chunk(c, carry):
        kt = kt_ref[pl.ds(pl.multiple_of(c * kc, kc), kc), :]
        d = _dot_nt(qh_sc[...].reshape(N_IDX * tq, LANES), kt)
        acc = jnp.zeros((tq, kc), F32)
        for h in range(N_IDX):
            acc = acc + _tile_lanes(wb_sc[h], slabs) * jnp.maximum(d[h * tq:(h + 1) * tq, :], 0.0)
        sc = jnp.where(c * kc + lane_k <= row_t, acc, -jnp.inf)
        key = _float_key(sc)
        key_sc[c] = key
        hi_sc[c] = _key_hi(key)
        return carry
    lax.fori_loop(0, n, chunk, 0)

    _topk_bias(key_sc, hi_sc, n, out_ref, row_t, kk=kk, R=tq, kc=kc)

    def fill(c, carry):
        out_ref[c] = jnp.full((tq, kc), NEG, out_ref.dtype)
        return carry
    lax.fori_loop(n, nc, fill, 0)


def _dsa_select(qi, tail, tail_bf, kk, tq=128, kc=512):
    S = qi.shape[0]
    kc = min(kc, S)
    nc = S // kc
    return pl.pallas_call(
        functools.partial(_dsa_select_kernel, tq=tq, kc=kc, nc=nc, kk=kk),
        grid=(S // tq,),
        in_specs=[pl.BlockSpec((tq, qi.shape[1]), lambda i: (i, 0)),
                  pl.BlockSpec((tq, LANES), lambda i: (i, 0)),
                  pl.BlockSpec((S, LANES), lambda i: (0, 0))],
        out_specs=pl.BlockSpec((nc, tq, kc), lambda i: (0, i, 0)),
        out_shape=jax.ShapeDtypeStruct((nc, S, kc), BF16),
        scratch_shapes=[pltpu.VMEM((nc, tq, kc), I32), pltpu.VMEM((nc, tq, kc), I16),
                        pltpu.VMEM((N_IDX, tq, LANES), BF16),
                        pltpu.VMEM((N_IDX, tq, LANES), F32)],
        compiler_params=_cparams(("parallel",)), name="dsa_select")(qi, tail, tail_bf)


def _select_rows_kernel(sc_ref, lim_ref, out_ref, key_sc, hi_sc, *, R, kc, nc, kk):
    lim = lim_ref[...]
    lane = lax.broadcasted_iota(I32, (R, kc), 1)

    def prep(c, carry):
        key = _float_key(jnp.where(c * kc + lane <= lim, sc_ref[c], -jnp.inf))
        key_sc[c] = key
        hi_sc[c] = _key_hi(key)
        return carry
    lax.fori_loop(0, nc, prep, 0)
    _topk_bias(key_sc, hi_sc, nc, out_ref, lim, kk=kk, R=R, kc=kc)


def _select_rows(scores_cm, lim, kk):
    nc, R, kc = scores_cm.shape
    return pl.pallas_call(
        functools.partial(_select_rows_kernel, R=R, kc=kc, nc=nc, kk=kk),
        grid=(1,),
        in_specs=[pl.BlockSpec((nc, R, kc), lambda i: (0, 0, 0)),
                  pl.BlockSpec((R, 1), lambda i: (0, 0))],
        out_specs=pl.BlockSpec((nc, R, kc), lambda i: (0, 0, 0)),
        out_shape=jax.ShapeDtypeStruct((nc, R, kc), BF16),
        scratch_shapes=[pltpu.VMEM((nc, R, kc), I32), pltpu.VMEM((nc, R, kc), I16)],
        compiler_params=_cparams(("arbitrary",)), name="select_rows")(scores_cm, lim)


def _online_softmax(s, m_prev, l_prev, reps):
    m_new = jnp.maximum(m_prev, jnp.max(s, axis=1, keepdims=True))
    alpha = jnp.exp2(m_prev - m_new)
    p = jnp.exp2(s - _tile_lanes(m_new, reps))
    l_new = alpha * l_prev + jnp.sum(p, axis=1, keepdims=True)
    return p, alpha, m_new, l_new


def _moba_top_blocks(gate, n_past):
    lane = lax.broadcasted_iota(I32, gate.shape, 1)
    g = jnp.where(lane < n_past, gate, -jnp.inf)
    ind = jnp.zeros(gate.shape, F32)
    for _ in range(MOBA_TOPK):
        mx = jnp.max(g, axis=1, keepdims=True)
        am = jnp.min(jnp.where(g == mx, lane, 2 * LANES), axis=1, keepdims=True)
        hit = jnp.where(mx > -jnp.inf, jnp.where(lane == am, 1.0, 0.0), 0.0)
        ind = ind + hit
        g = jnp.where(lane == am, -jnp.inf, g)
    return ind


def _block_expand(j, kc):
    r_i = lax.broadcasted_iota(I32, (LANES, kc), 0)
    c_i = lax.broadcasted_iota(I32, (LANES, kc), 1)
    return jnp.where(r_i == (j * kc + c_i) // MOBA_BLOCK, 1.0, 0.0).astype(BF16)


def _flash_kernel(*refs, mode, tq, kc, nj, npairs):
    i = pl.program_id(0)
    j = pl.program_id(1)
    q_ref, k_ref, v_ref = refs[0], refs[1], refs[2]
    pos = 3
    if mode == "bias":
        b_ref = refs[pos]; pos += 1
    elif mode == "moba":
        km_ref = refs[pos]; pos += 1
    elif mode == "diff":
        lam_ref = refs[pos]; pos += 1
    o_ref = refs[pos]; pos += 1
    m_sc, l_sc, acc_sc = refs[pos], refs[pos + 1], refs[pos + 2]
    if mode == "moba":
        sel_sc = refs[pos + 3]
    jmax = ((i + 1) * tq - 1) // kc
    jdiag = (i * tq) // kc
    reps = kc // LANES
    nblk = kc // MOBA_BLOCK
    lane = lax.broadcasted_iota(I32, (tq, LANES), 1)
    lo = lane < 64

    def q_maps(p, scale):
        q2 = q_ref[:, p * LANES:(p + 1) * LANES]
        if scale != 1.0:
            q2 = q2 * scale
        return (jnp.where(lo, q2, 0.0).astype(BF16), jnp.where(lo, 0.0, q2).astype(BF16))

    @pl.when(j == 0)
    def _():
        m_sc[...] = jnp.full(m_sc.shape, NEG, F32)
        l_sc[...] = jnp.zeros(l_sc.shape, F32)
        acc_sc[...] = jnp.zeros(acc_sc.shape, F32)
        if mode == "moba":
            n_past = (i * tq + lax.broadcasted_iota(I32, (tq, 1), 0)) // MOBA_BLOCK
            for p in range(npairs):
                km2 = km_ref[:, p * LANES:(p + 1) * LANES].astype(BF16)
                for e, qm in enumerate(q_maps(p, 1.0)):
                    sel_sc[2 * p + e] = _moba_top_blocks(_dot_nt(qm, km2), n_past)

    def step(diag):
        if mode == "moba":
            row_t = i * tq + lax.broadcasted_iota(I32, (tq, MOBA_BLOCK), 0)
            lane_b = lax.broadcasted_iota(I32, (tq, MOBA_BLOCK), 1)
        if mode == "bias":
            bias = b_ref[0].astype(F32)
        elif diag and mode == "diff":
            causal = (j * kc + lax.broadcasted_iota(I32, (tq, kc), 1)
                      <= i * tq + lax.broadcasted_iota(I32, (tq, kc), 0))
        for p in range(npairs):
            k2 = k_ref[:, p * LANES:(p + 1) * LANES]
            v2 = v_ref[:, p * LANES:(p + 1) * LANES]
            pv, al = [], []
            for e, qm in enumerate(q_maps(p, Q_SCALE)):
                h = 2 * p + e
                s = _dot_nt(qm, k2)
                if mode == "bias":
                    s = s + bias
                elif mode == "moba":
                    sel = sel_sc[h]
                    parts = []
                    for c in range(nblk):
                        blk = j * nblk + c
                        sb = s[:, c * MOBA_BLOCK:(c + 1) * MOBA_BLOCK]
                        picked = jnp.sum(jnp.where(lane == blk, sel, 0.0), axis=1, keepdims=True)
                        sc = sb + jnp.where(picked > 0.5, 0.0, NEG)
                        if diag:
                            own = jnp.where(blk * MOBA_BLOCK + lane_b <= row_t, sb, NEG)
                            sc = jnp.where(row_t // MOBA_BLOCK == blk, own, sc)
                        parts.append(sc)
                    s = jnp.concatenate(parts, axis=1) if nblk > 1 else parts[0]
                elif diag:
                    s = jnp.where(causal, s, NEG)
                pr, alpha, m_new, l_new = _online_softmax(s, m_sc[h], l_sc[h], reps)
                m_sc[h] = m_new
                l_sc[h] = l_new
                pv.append(_dot(pr.astype(BF16), v2))
                al.append(alpha)
            if mode == "diff":
                acc_sc[2 * p] = acc_sc[2 * p] * al[0] + pv[0]
                acc_sc[2 * p + 1] = acc_sc[2 * p + 1] * al[1] + pv[1]
            else:
                acc_sc[p] = acc_sc[p] * jnp.where(lo, al[0], al[1]) + jnp.where(lo, pv[0], pv[1])

    if mode == "bias":
        pl.when(j <= jmax)(lambda: step(False))
    else:
        pl.when(j < jdiag)(lambda: step(False))
        pl.when(jnp.logical_and(j >= jdiag, j <= jmax))(lambda: step(True))

    @pl.when(j == nj - 1)
    def _():
        for p in range(npairs):
            if mode == "diff":
                o = acc_sc[2 * p] / l_sc[2 * p] - lam_ref[0, 0] * (acc_sc[2 * p + 1] / l_sc[2 * p + 1])
            else:
                o = acc_sc[p] / jnp.where(lo, l_sc[2 * p], l_sc[2 * p + 1])
            o_ref[:, p * LANES:(p + 1) * LANES] = o


def _flash_tiles(mode, S):
    tq, kc = {"bias": (512, 512), "moba": (512, 1024), "diff": (512, 1024)}[mode]
    return min(tq, S), min(kc, S)


def _flash(q, k_bf, v_bf, mode, bias=None, kmean=None, lam=None):
    S, W = q.shape
    tq, kc = _flash_tiles(mode, S)
    assert kc % MOBA_BLOCK == 0 and tq % MOBA_BLOCK == 0
    ni, nj = S // tq, S // kc
    npairs = W // LANES
    jm = lambda i, j: jnp.minimum(j, ((i + 1) * tq - 1) // kc)
    in_specs = [pl.BlockSpec((tq, W), lambda i, j: (i, 0)),
                pl.BlockSpec((kc, W), lambda i, j: (jm(i, j), 0)),
                pl.BlockSpec((kc, W), lambda i, j: (jm(i, j), 0))]
    args = [q, k_bf, v_bf]
    nmaps = 2 * npairs
    scratch = [pltpu.VMEM((nmaps, tq, LANES), F32), pltpu.VMEM((nmaps, tq, LANES), F32),
               pltpu.VMEM((nmaps if mode == "diff" else npairs, tq, LANES), F32)]
    if mode == "bias":
        in_specs.append(pl.BlockSpec((1, tq, kc), lambda i, j: (jm(i, j), i, 0)))
        args.append(bias)
    elif mode == "moba":
        in_specs.append(pl.BlockSpec(kmean.shape, lambda i, j: (0, 0)))
        args.append(kmean)
        scratch.append(pltpu.VMEM((nmaps, tq, LANES), F32))
    elif mode == "diff":
        in_specs.append(pl.BlockSpec(memory_space=pltpu.SMEM))
        args.append(lam.reshape(1, 1))
    return pl.pallas_call(
        functools.partial(_flash_kernel, mode=mode, tq=tq, kc=kc, nj=nj, npairs=npairs),
        grid=(ni, nj), in_specs=in_specs,
        out_specs=pl.BlockSpec((tq, W), lambda i, j: (i, 0)),
        out_shape=jax.ShapeDtypeStruct((S, W), F32), scratch_shapes=scratch,
        compiler_params=_cparams(("parallel", "arbitrary"),
                                 VMEM_LIMIT_LARGE if mode == "diff" else VMEM_LIMIT),
        name="flash_" + mode)(*args)


def _block_mean_kernel(k_ref, o_ref, *, nb, per):
    i = pl.program_id(0)

    @pl.when(i < nb // per)
    def _():
        x = k_ref[...]
        o_ref[...] = jnp.mean(x.reshape(per, MOBA_BLOCK, x.shape[1]), axis=1)

    @pl.when(i >= nb // per)
    def _():
        o_ref[...] = jnp.zeros(o_ref.shape, F32)


def _block_means(kb):
    S, W = kb.shape
    nb, per = S // MOBA_BLOCK, 8
    last = nb // per - 1
    return pl.pallas_call(
        functools.partial(_block_mean_kernel, nb=nb, per=per),
        grid=(LANES // per,),
        in_specs=[pl.BlockSpec((per * MOBA_BLOCK, W), lambda i: (jnp.minimum(i, last), 0))],
        out_specs=pl.BlockSpec((per, W), lambda i: (i, 0)),
        out_shape=jax.ShapeDtypeStruct((LANES, W), F32),
        compiler_params=_cparams(("arbitrary",)), name="block_means")(kb)


def _page_specs(shape, ppc, nj):
    zeros = (0,) * len(shape)

    def make(p):
        return pl.BlockSpec((1,) + tuple(shape),
                            lambda b, j, pt: (pt[b, jnp.minimum(j, nj - 1) * ppc + p],) + zeros)
    return [make(p) for p in range(ppc)]


def _pool_t(cache):
    NP = cache.shape[0]
    nd = cache.ndim
    return jnp.transpose(cache, (0,) + tuple(range(2, nd)) + (1,)).reshape(NP, -1, PAGE)


DEC_CHUNK_PAGES = 4


def _decode_ppc(n_pages, want):
    ppc = min(want, n_pages)
    assert ppc % DEC_CHUNK_PAGES == 0 and n_pages % ppc == 0
    return ppc


def _idx_scores_kernel(pt_ref, q_ref, w_ref, new_ref, *refs, ppc, nj, T):
    j = pl.program_id(1)
    pages, o_ref = refs[:ppc], refs[ppc]
    cpp = DEC_CHUNK_PAGES
    kc = cpp * PAGE
    q = q_ref[0].astype(BF16)
    w = w_ref[0]

    def scores(kt):
        d = jnp.maximum(_dot(q, kt), 0.0) * _tile_lanes(w, kt.shape[1] // LANES)
        return jnp.sum(d.reshape(T, N_IDX, kt.shape[1]), axis=1)

    @pl.when(j < nj)
    def _():
        for c in range(ppc // cpp):
            kt = jnp.concatenate([r[0] for r in pages[c * cpp:(c + 1) * cpp]], axis=1).astype(BF16)
            o_ref[c, 0] = scores(kt)

    @pl.when(j == nj)
    def _():
        sc = scores(new_ref[0].astype(BF16))
        o_ref[0, 0] = jnp.concatenate([sc, jnp.zeros((T, kc - PAGE), F32)], axis=1)
        for c in range(1, ppc // cpp):
            o_ref[c, 0] = jnp.zeros((T, kc), F32)


def _idx_scores(pt, q_rows, w_rows, new_page, pool, ppc):
    B = q_rows.shape[0]
    T = q_rows.shape[1] // N_IDX
    nj = pt.shape[1] // ppc
    cps = ppc // DEC_CHUNK_PAGES
    kc = DEC_CHUNK_PAGES * PAGE
    grid_spec = pltpu.PrefetchScalarGridSpec(
        num_scalar_prefetch=1, grid=(B, nj + 1),
        in_specs=[pl.BlockSpec((1, T * N_IDX, HEAD_DIM), lambda b, j, pt: (b, 0, 0)),
                  pl.BlockSpec((1, T * N_IDX, LANES), lambda b, j, pt: (b, 0, 0)),
                  pl.BlockSpec((1, HEAD_DIM, PAGE), lambda b, j, pt: (b, 0, 0))]
        + _page_specs(pool.shape[1:], ppc, nj),
        out_specs=pl.BlockSpec((cps, 1, T, kc), lambda b, j, pt: (j, b, 0, 0)))
    return pl.pallas_call(
        functools.partial(_idx_scores_kernel, ppc=ppc, nj=nj, T=T),
        grid_spec=grid_spec, out_shape=jax.ShapeDtypeStruct(((nj + 1) * cps, B, T, kc), F32),
        compiler_params=_cparams(("parallel", "arbitrary")), name="idx_scores",
    )(pt, q_rows, w_rows, new_page, *([pool] * ppc))


def _pool_block_mean_kernel(pt_ref, *refs, npg):
    j = pl.program_id(1)
    pages, o_ref = refs[:npg], refs[npg]
    ppb = MOBA_BLOCK // PAGE
    lane = lax.broadcasted_iota(I32, o_ref.shape[1:], 1)

    @pl.when(j == 0)
    def _():
        o_ref[...] = jnp.zeros(o_ref.shape, F32)

    acc = o_ref[0]
    for blk in range(npg // ppb):
        tot = pages[blk * ppb][0]
        for a in range(1, ppb):
            tot = tot + pages[blk * ppb + a][0]
        mean = jnp.sum(tot, axis=1, keepdims=True) * (1.0 / MOBA_BLOCK)
        acc = jnp.where(lane == j * (npg // ppb) + blk, mean, acc)
    o_ref[0] = acc


def _pool_block_means(pt, pool_t, bpc=8):
    B, n_pages = pt.shape
    W = pool_t.shape[1]
    ppb = MOBA_BLOCK // PAGE
    nb = n_pages // ppb
    assert nb <= LANES
    bpc = min(bpc, nb)
    npg = bpc * ppb
    nj = nb // bpc
    grid_spec = pltpu.PrefetchScalarGridSpec(
        num_scalar_prefetch=1, grid=(B, nj),
        in_specs=_page_specs((W, PAGE), npg, nj),
        out_specs=pl.BlockSpec((1, W, LANES), lambda b, j, pt: (b, 0, 0)))
    return pl.pallas_call(
        functools.partial(_pool_block_mean_kernel, npg=npg),
        grid_spec=grid_spec, out_shape=jax.ShapeDtypeStruct((B, W, LANES), F32),
        compiler_params=_cparams(("parallel", "arbitrary")), name="pool_block_means",
    )(pt, *([pool_t] * npg))


def _paged_attn_kernel(pt_ref, *refs, mode, ppc, nj, T, G, W, Wv, vt):
    j = pl.program_id(1)
    q_ref, knew_ref, vnew_ref = refs[0], refs[1], refs[2]
    pos = 3
    if mode == "bias":
        b_ref = refs[pos]; pos += 1
    elif mode == "moba":
        km_ref = refs[pos]; pos += 1
    kpages = refs[pos:pos + ppc]; pos += ppc
    vpages = refs[pos:pos + ppc]; pos += ppc
    o_ref = refs[pos]; pos += 1
    m_sc, l_sc, acc_sc = refs[pos], refs[pos + 1], refs[pos + 2]
    if mode == "moba":
        sel_sc = refs[pos + 3]
    R = T * G
    kc = ppc * PAGE
    row = lax.broadcasted_iota(I32, (R, W), 0)
    col = lax.broadcasted_iota(I32, (R, W), 1)
    q = q_ref[0]
    qrep = jnp.concatenate([jnp.broadcast_to(q[t:t + 1], (G, W)) for t in range(T)], axis=0)
    qraw = jnp.where(col // HEAD_DIM == row % G, qrep, 0.0)
    qbd = (qraw * Q_SCALE).astype(BF16)

    @pl.when(j == 0)
    def _():
        m_sc[...] = jnp.full(m_sc.shape, NEG, F32)
        l_sc[...] = jnp.zeros(l_sc.shape, F32)
        acc_sc[...] = jnp.zeros(acc_sc.shape, F32)
        if mode == "moba":
            n_full = (nj * kc) // MOBA_BLOCK
            gate = _dot(qraw.astype(BF16), km_ref[0].astype(BF16))
            sel_sc[...] = _moba_top_blocks(gate, n_full).astype(BF16)

    def update(s, v):
        pr, alpha, m_new, l_new = _online_softmax(s, m_sc[...], l_sc[...], s.shape[1] // LANES)
        m_sc[...] = m_new
        l_sc[...] = l_new
        pv = _dot_nt(pr.astype(BF16), v) if vt else _dot(pr.astype(BF16), v)
        acc_sc[...] = acc_sc[...] * _tile_lanes(alpha, Wv // LANES) + pv

    @pl.when(j < nj)
    def _():
        k = jnp.concatenate([r[0] for r in kpages], axis=1).astype(BF16)
        v = jnp.concatenate([r[0] for r in vpages], axis=1 if vt else 0).astype(BF16)
        s = _dot(qbd, k)
        if mode == "bias":
            b = jnp.concatenate([b_ref[c, 0] for c in range(ppc // DEC_CHUNK_PAGES)], axis=1).astype(F32)
            s = s + jnp.concatenate([jnp.broadcast_to(b[t:t + 1], (G, kc)) for t in range(T)], axis=0)
        elif mode == "moba":
            s = jnp.where(_dot(sel_sc[...], _block_expand(j, kc)) > 0.5, s, NEG)
        update(s, v)

    @pl.when(j == nj)
    def _():
        s = _dot(qbd, knew_ref[0].astype(BF16))
        if mode == "bias":
            b = b_ref[0, 0][:, :PAGE].astype(F32)
            s = s + jnp.concatenate([jnp.broadcast_to(b[t:t + 1], (G, PAGE)) for t in range(T)], axis=0)
        else:
            r_i = lax.broadcasted_iota(I32, (R, PAGE), 0)
            c_i = lax.broadcasted_iota(I32, (R, PAGE), 1)
            s = jnp.where(c_i <= r_i // G, s, NEG)
        update(s, vnew_ref[0].astype(BF16))
        o = acc_sc[...] / _tile_lanes(l_sc[...], Wv // LANES)
        rv = lax.broadcasted_iota(I32, (R, Wv), 0)
        cv = lax.broadcasted_iota(I32, (R, Wv), 1)
        o = jnp.where(cv // HEAD_DIM == rv % G, o, 0.0)
        o_ref[0] = jnp.sum(o.reshape(T, G, Wv), axis=1)


def _paged_diff_kernel(pt_ref, q_ref, knew_ref, vnew_ref, lam_ref, *refs, ppc, nj, T, H, W):
    j = pl.program_id(1)
    kpages, vpages = refs[:ppc], refs[ppc:2 * ppc]
    o_ref, m_sc, l_sc, acc_sc = refs[2 * ppc:2 * ppc + 4]
    G = 2 * H
    R = G * T
    RH = 2 * T
    row = lax.broadcasted_iota(I32, (R, W), 0)
    col = lax.broadcasted_iota(I32, (R, W), 1)
    qrep = jnp.concatenate([q_ref[0]] * G, axis=0)
    qbd = (jnp.where(col // HEAD_DIM == row // T, qrep, 0.0) * Q_SCALE).astype(BF16)

    @pl.when(j == 0)
    def _():
        m_sc[...] = jnp.full(m_sc.shape, NEG, F32)
        l_sc[...] = jnp.zeros(l_sc.shape, F32)
        acc_sc[...] = jnp.zeros(acc_sc.shape, F32)

    def update(s, vrefs):
        pr, alpha, m_new, l_new = _online_softmax(s, m_sc[...], l_sc[...], s.shape[1] // LANES)
        m_sc[...] = m_new
        l_sc[...] = l_new
        prb = pr.astype(BF16)
        for h in range(H):
            vh = jnp.concatenate([r[0, pl.ds(h, PAGE, stride=H), :] for r in vrefs],
                                 axis=0).astype(BF16)
            rows = slice(h * RH, (h + 1) * RH)
            acc_sc[rows, :] = acc_sc[rows, :] * alpha[rows, :] + _dot(prb[rows, :], vh)

    @pl.when(j < nj)
    def _():
        k = jnp.concatenate([r[0] for r in kpages], axis=1).astype(BF16)
        update(_dot(qbd, k), vpages)

    @pl.when(j == nj)
    def _():
        s = _dot(qbd, knew_ref[0].astype(BF16))
        r_i = lax.broadcasted_iota(I32, (R, PAGE), 0)
        c_i = lax.broadcasted_iota(I32, (R, PAGE), 1)
        update(jnp.where(c_i <= r_i % T, s, NEG), [vnew_ref])
        o = acc_sc[...] / l_sc[...]
        outs = []
        for h in range(H):
            outs.append(o[h * RH:h * RH + T, :] - lam_ref[0, 0] * o[h * RH + T:(h + 1) * RH, :])
        o_ref[0] = jnp.concatenate(outs, axis=1)


def _paged_diff(pt, q3, knew, vnew, pool_k, pool_v, lam, ppc):
    B, T, W = q3.shape
    E = pool_v.shape[2]
    H = pool_v.shape[1] // PAGE
    assert E == LANES and W == 2 * H * HEAD_DIM and (2 * T) % 8 == 0
    nj = pt.shape[1] // ppc
    cmap3 = lambda b, j, pt: (b, 0, 0)
    R = 2 * H * T
    grid_spec = pltpu.PrefetchScalarGridSpec(
        num_scalar_prefetch=1, grid=(B, nj + 1),
        in_specs=[pl.BlockSpec((1, T, W), cmap3), pl.BlockSpec((1, W, PAGE), cmap3),
                  pl.BlockSpec((1, PAGE * H, E), cmap3), pl.BlockSpec(memory_space=pltpu.SMEM)]
        + _page_specs(pool_k.shape[1:], ppc, nj) + _page_specs(pool_v.shape[1:], ppc, nj),
        out_specs=pl.BlockSpec((1, T, H * E), cmap3),
        scratch_shapes=[pltpu.VMEM((R, LANES), F32), pltpu.VMEM((R, LANES), F32), pltpu.VMEM((R, E), F32)])
    return pl.pallas_call(
        functools.partial(_paged_diff_kernel, ppc=ppc, nj=nj, T=T, H=H, W=W),
        grid_spec=grid_spec, out_shape=jax.ShapeDtypeStruct((B, T, H * E), F32),
        compiler_params=_cparams(("parallel", "arbitrary")), name="paged_diff",
    )(pt, q3, knew, vnew, lam.reshape(1, 1), *([pool_k] * ppc), *([pool_v] * ppc))


def _paged_attn(pt, q3, knew, vnew, pool_k, pool_v, mode, vt, ppc, bias=None, kmean=None):
    B, T, W = q3.shape
    Wv = pool_v.shape[1] if vt else pool_v.shape[2]
    G = W // HEAD_DIM
    nj = pt.shape[1] // ppc
    cps = ppc // DEC_CHUNK_PAGES
    cmap = lambda b, j, pt: (b, 0, 0)
    in_specs = [pl.BlockSpec((1, T, W), cmap), pl.BlockSpec((1,) + knew.shape[1:], cmap),
                pl.BlockSpec((1,) + vnew.shape[1:], cmap)]
    args = [q3, knew, vnew]
    scratch = [pltpu.VMEM((T * G, LANES), F32), pltpu.VMEM((T * G, LANES), F32),
               pltpu.VMEM((T * G, Wv), F32)]
    if mode == "bias":
        assert bias.shape[0] == (nj + 1) * cps
        in_specs.append(pl.BlockSpec((cps, 1, T, bias.shape[3]), lambda b, j, pt: (j, b, 0, 0)))
        args.append(bias)
    elif mode == "moba":
        in_specs.append(pl.BlockSpec((1, W, LANES), cmap))
        args.append(kmean)
        scratch.append(pltpu.VMEM((T * G, LANES), BF16))
    in_specs += _page_specs(pool_k.shape[1:], ppc, nj) + _page_specs(pool_v.shape[1:], ppc, nj)
    args += [pool_k] * ppc + [pool_v] * ppc
    grid_spec = pltpu.PrefetchScalarGridSpec(
        num_scalar_prefetch=1, grid=(B, nj + 1), in_specs=in_specs,
        out_specs=pl.BlockSpec((1, T, Wv), cmap), scratch_shapes=scratch)
    return pl.pallas_call(
        functools.partial(_paged_attn_kernel, mode=mode, ppc=ppc, nj=nj, T=T, G=G, W=W, Wv=Wv, vt=vt),
        grid_spec=grid_spec, out_shape=jax.ShapeDtypeStruct((B, T, Wv), F32),
        compiler_params=_cparams(("parallel", "arbitrary")), name="paged_" + mode)(pt, *args)


def _lambda_kernel(a_ref, o_ref, *, lam_init):
    a = a_ref[...]
    d1 = jnp.sum(a[0:1] * a[1:2], axis=1, keepdims=True)
    d2 = jnp.sum(a[2:3] * a[3:4], axis=1, keepdims=True)
    o_ref[...] = jnp.exp(d1) - jnp.exp(d2) + lam_init


def _diff_lambda(lq1, lk1, lq2, lk2, lam_init):
    a = jnp.zeros((8, LANES), F32).at[:4, :HEAD_DIM].set(jnp.stack([lq1, lk1, lq2, lk2]))
    return pl.pallas_call(functools.partial(_lambda_kernel, lam_init=lam_init),
                          out_shape=jax.ShapeDtypeStruct((1, 1), F32), name="diff_lambda")(a)


def _pad_new_page(x3):
    B, T, W = x3.shape
    return jnp.pad(x3, ((0, 0), (0, PAGE - T), (0, 0)))


def _new_page_t(x3):
    B, T, W = x3.shape
    return jnp.pad(jnp.swapaxes(x3, 1, 2), ((0, 0), (0, 0), (0, PAGE - T)))


def kernel(x_prompt, x_sample, cache_a_k, cache_a_v, cache_idx_k, cache_b_k, cache_b_v, cache_c_k, cache_c_v, cache_mem_k, cache_mem_v, page_table, mem_prompt, g_mix, g_mem, g_final, w_in_even, w_out_even, w_in_odd, w_out_odd, lam_q1, lam_k1, lam_q2, lam_k2, g_subln, w_mq, w_mk, w_mv, w_mo):
    Bp, S, D = x_prompt.shape
    Bs, T, _ = x_sample.shape
    assert Bp == 1
    depth = g_mix.shape[0]
    n_pages = page_table.shape[1]
    past = n_pages * PAGE
    assert past % MOBA_BLOCK == 0 and S % MOBA_BLOCK == 0
    NP = cache_a_k.shape[1]
    HA = cache_a_k.shape[3]
    HB = cache_b_k.shape[3]
    HC = cache_c_k.shape[3]
    HM, HDM = cache_mem_k.shape[3], cache_mem_k.shape[4]
    NM = cache_mem_k.shape[2]
    WA, WB, WC, WM = HA * HEAD_DIM, HB * HEAD_DIM, HC * 2 * HEAD_DIM, HM * HDM
    WIQ = N_IDX * HEAD_DIM
    Ms = Bs * T

    cos_p, sin_p = _rope_tables(jnp.arange(S))
    cos_s, sin_s = _rope_tables(jnp.tile(past + jnp.arange(T), Bs))
    xp = x_prompt.reshape(S, D)
    xs = x_sample.reshape(Ms, D)
    mem2 = mem_prompt.reshape(NM, D)
    pt = page_table.astype(I32)

    n_tail = 4 * WA + 4 * WB + WIQ
    even_groups = []
    for gi in range(8):
        kind = "rope" if gi in (0, 1, 4, 5) else "plain"
        even_groups.append((gi * WA, WA, kind, 1.0, gi in (1, 2, 5, 6)))
    even_groups.append((8 * WA, WIQ, "rope", 1.0, False))
    even_groups.append((n_tail, LANES, "tail", float(WIQ) ** -0.5, True))
    odd_groups = [(0, WC, "rope", 1.0, False), (WC, WC, "rope", 1.0, True),
                  (2 * WC, WC, "plain", 1.0, True), (3 * WC, WC, "plain", 1.0, False)]
    mem_groups = [(0, WM, "plain", 1.0, False), (WM, WM, "plain", 1.0, False)]

    outs = {}
    ones = jnp.ones((D,), F32)
    for l in range(depth):
        i = l // 2
        if l % 2 == 0:
            w_in = w_in_even[i]
            w_in = jnp.pad(w_in, ((0, 0), (0, n_tail + LANES - w_in.shape[1]))).astype(BF16)
            w_out = w_out_even[i].astype(BF16)
            (qa, ka, ka_b, va, va_b, ga, qb, kb, kb_b, vb, vb_b, gb, qi, tail, tail_b) = _rms_proj(
                xp, g_mix[l], w_in, even_groups, cos_p, sin_p)
            kk = min(TOPK_TOK, S // 4)
            bias = _dsa_select(qi, tail, tail_b, kk)
            oa = _flash(qa, ka_b, va_b, "bias", bias=bias)
            ob = _flash(qb, kb_b, vb_b, "moba", kmean=_block_means(kb))
            xp = _gated_out(xp, w_out, [(oa, ga), (ob, gb)])
            outs.setdefault("pa_k", []).append(ka.reshape(1, S, HA, HEAD_DIM))
            outs.setdefault("pa_v", []).append(va.reshape(1, S, HA, HEAD_DIM))
            outs.setdefault("pidx_k", []).append(tail[:, :HEAD_DIM].reshape(1, S, HEAD_DIM))
            outs.setdefault("pb_k", []).append(kb.reshape(1, S, HB, HEAD_DIM))
            outs.setdefault("pb_v", []).append(vb.reshape(1, S, HB, HEAD_DIM))
            (qa, ka, _, va, _, ga, qb, kb, _, vb, _, gb, qi, tail, _) = _rms_proj(
                xs, g_mix[l], w_in, even_groups, cos_s, sin_s)
            q_rows = qi.reshape(Bs, T * N_IDX, HEAD_DIM)
            w_rows = jnp.broadcast_to(tail[:, HEAD_DIM:HEAD_DIM + N_IDX].reshape(Bs, T * N_IDX, 1),
                                      (Bs, T * N_IDX, LANES))
            tail3 = tail.reshape(Bs, T, LANES)
            ki_new = _new_page_t(tail3[:, :, :HEAD_DIM])
            ppc = _decode_ppc(n_pages, 16)
            sc = _idx_scores(pt, q_rows, w_rows, ki_new, _pool_t(cache_idx_k[i]), ppc)
            ncs, _, _, kcs = sc.shape
            lim = (past + jnp.tile(jnp.arange(T, dtype=I32), Bs)).reshape(Ms, 1)
            bias_s = _select_rows(sc.reshape(ncs, Ms, kcs), lim, min(TOPK_TOK, (past + T) // 4))
            oa = _paged_attn(pt, qa.reshape(Bs, T, WA), _new_page_t(ka.reshape(Bs, T, WA)),
                             _new_page_t(va.reshape(Bs, T, WA)),
                             _pool_t(cache_a_k[i]), _pool_t(cache_a_v[i]),
                             "bias", True, ppc, bias=bias_s.reshape(ncs, Bs, T, kcs))
            pool_bk = _pool_t(cache_b_k[i])
            ob = _paged_attn(pt, qb.reshape(Bs, T, WB), _new_page_t(kb.reshape(Bs, T, WB)),
                             _new_page_t(vb.reshape(Bs, T, WB)),
                             pool_bk, _pool_t(cache_b_v[i]),
                             "moba", True, ppc, kmean=_pool_block_means(pt, pool_bk, 16))
            xs = _gated_out(xs, w_out, [(oa.reshape(Ms, WA), ga), (ob.reshape(Ms, WB), gb)])
            outs.setdefault("sa_k", []).append(ka.reshape(Bs, T, HA, HEAD_DIM))
            outs.setdefault("sa_v", []).append(va.reshape(Bs, T, HA, HEAD_DIM))
            outs.setdefault("sidx_k", []).append(tail[:, :HEAD_DIM].reshape(Bs, T, HEAD_DIM))
            outs.setdefault("sb_k", []).append(kb.reshape(Bs, T, HB, HEAD_DIM))
            outs.setdefault("sb_v", []).append(vb.reshape(Bs, T, HB, HEAD_DIM))
        else:
            lam_init = 0.8 - 0.6 * math.exp(-0.3 * l)
            lam = _diff_lambda(lam_q1[i], lam_k1[i], lam_q2[i], lam_k2[i], lam_init)
            w_in = w_in_odd[i].astype(BF16)
            w_out = w_out_odd[i].astype(BF16)
            q, k, k_b, v, v_b, g = _rms_proj(xp, g_mix[l], w_in, odd_groups, cos_p, sin_p)
            o = _flash(q, k_b, v_b, "diff", lam=lam)
            xp = _gated_out(xp, w_out, [(o, g)], gsub=g_subln[i], post_scale=1.0 - lam_init)
            outs.setdefault("pc_k", []).append(k.reshape(1, S, HC, 2, HEAD_DIM))
            outs.setdefault("pc_v", []).append(v.reshape(1, S, HC, 2 * HEAD_DIM))
            q, k, _, v, _, g = _rms_proj(xs, g_mix[l], w_in, odd_groups, cos_s, sin_s)
            o = _paged_diff(pt, q.reshape(Bs, T, WC), _new_page_t(k.reshape(Bs, T, WC)),
                            _pad_new_page(v.reshape(Bs, T, WC)).reshape(Bs, PAGE * HC, 2 * HEAD_DIM),
                            _pool_t(cache_c_k[i]), cache_c_v[i].reshape(NP, PAGE * HC, 2 * HEAD_DIM), lam,
                            _decode_ppc(n_pages, 8))
            xs = _gated_out(xs, w_out, [(o.reshape(Ms, WC), g)], gsub=g_subln[i], post_scale=1.0 - lam_init)
            outs.setdefault("sc_k", []).append(k.reshape(Bs, T, HC, 2, HEAD_DIM))
            outs.setdefault("sc_v", []).append(v.reshape(Bs, T, HC, 2 * HEAD_DIM))
        w_kv = jnp.concatenate([w_mk[l], w_mv[l]], axis=1).astype(BF16)
        mk, mv = _rms_proj(mem2, ones, w_kv, mem_groups, norm=False)
        w_q = w_mq[l].astype(BF16)
        w_o = w_mo[l].astype(BF16)
        last = l == depth - 1
        q, g = _rms_proj(xp, g_mem[l], w_q, mem_groups)
        o = _mem_attn(q.reshape(1, S, WM), mk.reshape(1, NM, WM), mv.reshape(1, NM, WM), HM, HDM)
        res = _gated_out(xp, w_o, [(o.reshape(S, WM), g)], gfin=g_final if last else None)
        xp, yp = res if last else (res, None)
        q, g = _rms_proj(xs, g_mem[l], w_q, mem_groups)
        o = _mem_attn(q.reshape(Bs, T, WM), cache_mem_k[l].reshape(Bs, NM, WM),
                      cache_mem_v[l].reshape(Bs, NM, WM), HM, HDM)
        res = _gated_out(xs, w_o, [(o.reshape(Ms, WM), g)], gfin=g_final if last else None)
        xs, ys = res if last else (res, None)
        outs.setdefault("pm_k", []).append(mk.reshape(1, NM, HM, HDM))
        outs.setdefault("pm_v", []).append(mv.reshape(1, NM, HM, HDM))

    st = lambda name: jnp.stack(outs[name])
    return (yp.reshape(1, S, D), ys.reshape(Bs, T, D),
            st("pa_k"), st("pa_v"), st("pidx_k"), st("pb_k"), st("pb_v"), st("pc_k"), st("pc_v"),
            st("pm_k"), st("pm_v"),
            st("sa_k"), st("sa_v"), st("sidx_k"), st("sb_k"), st("sb_v"), st("sc_k"), st("sc_v"))
```

```python
import functools
import math

import jax
import jax.numpy as jnp
from jax import lax
from jax.experimental import pallas as pl
from jax.experimental.pallas import tpu as pltpu

F32 = jnp.float32
BF16 = jnp.bfloat16
I32 = jnp.int32

HEAD_DIM = 64
PAGE = 128
N_IDX = 8
TOPK_TOK = 256
MOBA_BLOCK = 256
MOBA_TOPK = 3
ROPE_THETA = 10000.0
EPS = 1e-6
LANES = 128
NEG = -1e30
Q_SCALE = HEAD_DIM ** -0.5 * math.log2(math.e)
INT_MIN = -2147483648
KEY_NEG_INF = -2139095041
VMEM_LIMIT = 56 * 1024 * 1024
VMEM_LIMIT_LARGE = 60 * 1024 * 1024


def _cparams(sem, vmem=VMEM_LIMIT):
    return pltpu.CompilerParams(dimension_semantics=sem, vmem_limit_bytes=vmem)


def _dot_nt(a, b):
    return lax.dot_general(a, b, (((1,), (1,)), ((), ())), preferred_element_type=F32)


def _dot(a, b):
    return jnp.dot(a, b, preferred_element_type=F32)


def _tile_lanes(x, reps):
    return x if reps == 1 else jnp.concatenate([x] * reps, axis=1)


def _rope_tables(pos):
    inv = ROPE_THETA ** (-jnp.arange(0, HEAD_DIM, 2, dtype=F32) / HEAD_DIM)
    ang = pos.astype(F32)[:, None] * inv[None, :]
    cos, sin = jnp.cos(ang), jnp.sin(ang)
    cos128 = jnp.concatenate([cos] * 4, axis=1)
    sin128 = jnp.concatenate([-sin, sin, -sin, sin], axis=1)
    return cos128, sin128


def _rope128(z, c, s):
    lane = lax.broadcasted_iota(I32, z.shape, 1)
    first = (lane & 63) < 32
    partner = jnp.where(first, pltpu.roll(z, 96, axis=1), pltpu.roll(z, 32, axis=1))
    return z * c + partner * s


def _proj_kernel(*refs, groups, norm, has_rope):
    x_ref, g_ref, w_ref = refs[0], refs[1], refs[2]
    pos = 3
    if has_rope:
        c = refs[3][...]
        s = refs[4][...]
        pos = 5
    outs = refs[pos:]
    x = x_ref[...]
    if norm:
        x = x * lax.rsqrt(jnp.mean(x * x, axis=-1, keepdims=True) + EPS) * g_ref[...]
    xb = x.astype(BF16)
    oi = 0
    for start, width, kind, scale, dup in groups:
        z = _dot(xb, w_ref[:, start:start + width])
        if kind == "rope":
            z = jnp.concatenate(
                [_rope128(z[:, a:a + LANES], c, s) for a in range(0, width, LANES)], axis=1)
        elif kind == "tail":
            lane = lax.broadcasted_iota(I32, z.shape, 1)
            z = jnp.where(lane < 64, _rope128(z, c, s),
                          jnp.where(lane < 64 + N_IDX, z * scale, 0.0))
        outs[oi][...] = z
        oi += 1
        if dup:
            outs[oi][...] = z.astype(BF16)
            oi += 1


def _rms_proj(x, g, w, groups, cos=None, sin=None, norm=True):
    M, D = x.shape
    tm = min(256, M)
    has_rope = cos is not None
    in_specs = [pl.BlockSpec((tm, D), lambda i: (i, 0)),
                pl.BlockSpec((1, D), lambda i: (0, 0)),
                pl.BlockSpec(w.shape, lambda i: (0, 0))]
    args = [x, g.reshape(1, D), w]
    if has_rope:
        in_specs += [pl.BlockSpec((tm, LANES), lambda i: (i, 0))] * 2
        args += [cos, sin]
    out_shape, out_specs = [], []
    for _, width, _, _, dup in groups:
        out_shape.append(jax.ShapeDtypeStruct((M, width), F32))
        out_specs.append(pl.BlockSpec((tm, width), lambda i: (i, 0)))
        if dup:
            out_shape.append(jax.ShapeDtypeStruct((M, width), BF16))
            out_specs.append(pl.BlockSpec((tm, width), lambda i: (i, 0)))
    return pl.pallas_call(
        functools.partial(_proj_kernel, groups=tuple(groups), norm=norm, has_rope=has_rope),
        grid=(M // tm,), in_specs=in_specs, out_specs=out_specs, out_shape=out_shape,
        compiler_params=_cparams(("parallel",)), name="rms_proj")(*args)


def _silu(g):
    return g * (1.0 / (1.0 + jnp.exp(-g)))


def _gated_out_kernel(*refs, widths, subnorm, post_scale, final_norm):
    resid_ref, w_ref = refs[0], refs[1]
    pos = 2
    if subnorm:
        gsub = refs[pos][...]
        pos += 1
    if final_norm:
        gfin = refs[pos][...]
        pos += 1
    acc = resid_ref[...]
    off = 0
    for wd in widths:
        o = refs[pos][...]
        g = refs[pos + 1][...]
        pos += 2
        if subnorm:
            parts = []
            for a in range(0, wd, LANES):
                oh = o[:, a:a + LANES]
                parts.append(oh * lax.rsqrt(jnp.mean(oh * oh, axis=-1, keepdims=True) + EPS)
                             * gsub * post_scale)
            o = jnp.concatenate(parts, axis=1)
        z = (o * _silu(g)).astype(BF16)
        acc = acc + _dot(z, w_ref[off:off + wd, :])
        off += wd
    out_ref = refs[pos]
    out_ref[...] = acc
    if final_norm:
        refs[pos + 1][...] = acc * lax.rsqrt(jnp.mean(acc * acc, axis=-1, keepdims=True) + EPS) * gfin


def _gated_out(resid, w, parts, gsub=None, post_scale=1.0, gfin=None):
    M, D = resid.shape
    tm = min(256, M)
    widths = tuple(o.shape[1] for o, _ in parts)
    in_specs = [pl.BlockSpec((tm, D), lambda i: (i, 0)), pl.BlockSpec(w.shape, lambda i: (0, 0))]
    args = [resid, w]
    if gsub is not None:
        in_specs.append(pl.BlockSpec((1, LANES), lambda i: (0, 0)))
        args.append(gsub.reshape(1, LANES))
    if gfin is not None:
        in_specs.append(pl.BlockSpec((1, D), lambda i: (0, 0)))
        args.append(gfin.reshape(1, D))
    for o, g in parts:
        wd = o.shape[1]
        in_specs += [pl.BlockSpec((tm, wd), lambda i: (i, 0))] * 2
        args += [o, g]
    out_shape = [jax.ShapeDtypeStruct((M, D), F32)]
    out_specs = [pl.BlockSpec((tm, D), lambda i: (i, 0))]
    if gfin is not None:
        out_shape.append(jax.ShapeDtypeStruct((M, D), F32))
        out_specs.append(pl.BlockSpec((tm, D), lambda i: (i, 0)))
    res = pl.pallas_call(
        functools.partial(_gated_out_kernel, widths=widths, subnorm=gsub is not None,
                          post_scale=post_scale, final_norm=gfin is not None),
        grid=(M // tm,), in_specs=in_specs, out_specs=out_specs, out_shape=out_shape,
        compiler_params=_cparams(("parallel",)), name="gated_out")(*args)
    return res if gfin is not None else res[0]


def _mem_attn_kernel(q_ref, k_ref, v_ref, o_ref, *, heads, hd):
    scale = hd ** -0.5
    outs = []
    for h in range(heads):
        q = q_ref[0, :, h * hd:(h + 1) * hd].astype(BF16)
        k = k_ref[0, :, h * hd:(h + 1) * hd].astype(BF16)
        v = v_ref[0, :, h * hd:(h + 1) * hd].astype(BF16)
        s = _dot_nt(q, k) * scale
        m = jnp.max(s, axis=1, keepdims=True)
        p = jnp.exp(s - m)
        l = jnp.sum(p, axis=1, keepdims=True)
        outs.append(_dot((p / l).astype(BF16), v))
    o_ref[0] = jnp.concatenate(outs, axis=1)


def _mem_attn(q3, mk3, mv3, heads, hd):
    B, T, W = q3.shape
    Bk, N, _ = mk3.shape
    tq = min(256, T)
    kidx = (lambda b, i: (b, 0, 0)) if Bk == B else (lambda b, i: (0, 0, 0))
    return pl.pallas_call(
        functools.partial(_mem_attn_kernel, heads=heads, hd=hd),
        grid=(B, T // tq),
        in_specs=[pl.BlockSpec((1, tq, W), lambda b, i: (b, i, 0)),
                  pl.BlockSpec((1, N, W), kidx), pl.BlockSpec((1, N, W), kidx)],
        out_specs=pl.BlockSpec((1, tq, W), lambda b, i: (b, i, 0)),
        out_shape=jax.ShapeDtypeStruct((B, T, W), F32),
        compiler_params=_cparams(("parallel", "parallel")), name="mem_attn")(q3, mk3, mv3)


def _float_key(x):
    bits = lax.bitcast_convert_type(x, I32)
    return jnp.where(bits < 0, bits ^ 0x7FFFFFFF, bits)


def _topk_bias(key_ref, n, out_ref, lim, *, kk, R, kc):
    slabs = kc // LANES

    def count(thr_b, strict):
        def body(c, acc):
            kt = key_ref[c]
            for a in range(slabs):
                blk = kt[:, a * LANES:(a + 1) * LANES]
                hit = (blk > thr_b) if strict else (blk >= thr_b)
                acc = acc + jnp.where(hit, 1, 0)
            return acc
        acc = lax.fori_loop(0, n, body, jnp.zeros((R, LANES), I32))
        return jnp.sum(acc, axis=1, keepdims=True)

    def search(state):
        b, t_u, done, _ = state
        cand_u = t_u | jnp.left_shift(jnp.int32(1), 31 - b)
        cnt = count(cand_u ^ INT_MIN, False)
        t_new = jnp.where(done > 0, t_u, jnp.where(cnt >= kk, cand_u, t_u))
        done = jnp.where(cnt == kk, 1, done)
        return b + 1, t_new, done, jnp.min(done)

    zeros = jnp.zeros((R, LANES), I32)
    _, t_u, _, _ = lax.while_loop(lambda st: jnp.logical_and(st[0] < 32, st[3] == 0), search,
                                  (jnp.int32(0), zeros, zeros, jnp.int32(0)))
    thr = t_u ^ INT_MIN
    cnt_ge = count(thr, False)
    surplus = jnp.where(cnt_ge > kk, jnp.where(thr[:, :1] > KEY_NEG_INF, 1, 0), 0)
    has_ties = jnp.max(surplus)
    thr_full = _tile_lanes(thr, slabs)
    lane = lax.broadcasted_iota(I32, (R, kc), 1)

    @pl.when(has_ties == 0)
    def _():
        def body(c, carry):
            sel = jnp.where(key_ref[c] >= thr_full, 0.0, NEG)
            out_ref[c] = jnp.where(c * kc + lane <= lim, sel, NEG).astype(out_ref.dtype)
            return carry
        lax.fori_loop(0, n, body, 0)

    @pl.when(has_ties != 0)
    def _():
        r_i = lax.broadcasted_iota(I32, (kc, kc), 0)
        c_i = lax.broadcasted_iota(I32, (kc, kc), 1)
        upper = jnp.where(r_i <= c_i, 1.0, 0.0).astype(BF16)
        need_f = (kk - count(thr, True)).astype(F32)

        def body(c, run):
            kt = key_ref[c]
            eq = jnp.where(kt == thr_full, 1.0, 0.0)
            rank = run + _dot(eq.astype(BF16), upper)
            take_eq = jnp.where(rank <= need_f, eq, 0.0)
            sel = jnp.where(kt > thr_full, 0.0, jnp.where(take_eq > 0.5, 0.0, NEG))
            out_ref[c] = jnp.where(c * kc + lane <= lim, sel, NEG).astype(out_ref.dtype)
            return run + jnp.sum(eq, axis=1, keepdims=True)
        lax.fori_loop(0, n, body, jnp.zeros((R, 1), F32))


def _dsa_select_kernel(qi_ref, wq_ref, kt_ref, out_ref, key_sc, qh_sc, wb_sc, *, tq, kc, nc, kk):
    i = pl.program_id(0)
    t0 = i * tq
    n = (t0 + tq + kc - 1) // kc
    lane = lax.broadcasted_iota(I32, (tq, LANES), 1)
    q = qi_ref[...]
    wq = wq_ref[...]
    for h in range(N_IDX):
        slab = q[:, (h // 2) * LANES:(h // 2 + 1) * LANES]
        if h % 2 == 1:
            slab = pltpu.roll(slab, 64, axis=1)
        qh_sc[h] = jnp.where(lane < 64, slab, 0.0).astype(BF16)
        wcol = jnp.sum(jnp.where(lane == 64 + h, wq, 0.0), axis=1, keepdims=True)
        wb_sc[h] = jnp.broadcast_to(wcol, (tq, LANES))
    row_t = t0 + lax.broadcasted_iota(I32, (tq, 1), 0)
    lane_k = lax.broadcasted_iota(I32, (tq, kc), 1)
    slabs = kc // LANES

    def chunk(c, carry):
        kt = kt_ref[pl.ds(pl.multiple_of(c * kc, kc), kc), :]
        acc = jnp.zeros((tq, kc), F32)
        for h in range(N_IDX):
            d = _dot_nt(qh_sc[h], kt)
            acc = acc + _tile_lanes(wb_sc[h], slabs) * jnp.maximum(d, 0.0)
        sc = jnp.where(c * kc + lane_k <= row_t, acc, -jnp.inf)
        key_sc[c] = _float_key(sc)
        return carry
    lax.fori_loop(0, n, chunk, 0)

    _topk_bias(key_sc, n, out_ref, row_t, kk=kk, R=tq, kc=kc)

    def fill(c, carry):
        out_ref[c] = jnp.full((tq, kc), NEG, out_ref.dtype)
        return carry
    lax.fori_loop(n, nc, fill, 0)


def _dsa_select(qi, tail, tail_bf, kk, kc, tq=128):
    S = qi.shape[0]
    nc = S // kc
    return pl.pallas_call(
        functools.partial(_dsa_select_kernel, tq=tq, kc=kc, nc=nc, kk=kk),
        grid=(S // tq,),
        in_specs=[pl.BlockSpec((tq, qi.shape[1]), lambda i: (i, 0)),
                  pl.BlockSpec((tq, LANES), lambda i: (i, 0)),
                  pl.BlockSpec((S, LANES), lambda i: (0, 0))],
        out_specs=pl.BlockSpec((nc, tq, kc), lambda i: (0, i, 0)),
        out_shape=jax.ShapeDtypeStruct((nc, S, kc), BF16),
        scratch_shapes=[pltpu.VMEM((nc, tq, kc), I32),
                        pltpu.VMEM((N_IDX, tq, LANES), BF16),
                        pltpu.VMEM((N_IDX, tq, LANES), F32)],
        compiler_params=_cparams(("parallel",)), name="dsa_select")(qi, tail, tail_bf)


def _select_rows_kernel(sc_ref, lim_ref, out_ref, key_sc, *, R, kc, nc, kk):
    lim = lim_ref[...]
    lane = lax.broadcasted_iota(I32, (R, kc), 1)

    def prep(c, carry):
        key_sc[c] = _float_key(jnp.where(c * kc + lane <= lim, sc_ref[c], -jnp.inf))
        return carry
    lax.fori_loop(0, nc, prep, 0)
    _topk_bias(key_sc, nc, out_ref, lim, kk=kk, R=R, kc=kc)


def _select_rows(scores_cm, lim, kk):
    nc, R, kc = scores_cm.shape
    return pl.pallas_call(
        functools.partial(_select_rows_kernel, R=R, kc=kc, nc=nc, kk=kk),
        grid=(1,),
        in_specs=[pl.BlockSpec((nc, R, kc), lambda i: (0, 0, 0)),
                  pl.BlockSpec((R, 1), lambda i: (0, 0))],
        out_specs=pl.BlockSpec((nc, R, kc), lambda i: (0, 0, 0)),
        out_shape=jax.ShapeDtypeStruct((nc, R, kc), BF16),
        scratch_shapes=[pltpu.VMEM((nc, R, kc), I32)],
        compiler_params=_cparams(("arbitrary",)), name="select_rows")(scores_cm, lim)


def _online_softmax(s, m_prev, l_prev, reps):
    m_new = jnp.maximum(m_prev, jnp.max(s, axis=1, keepdims=True))
    alpha = jnp.exp2(m_prev - m_new)
    p = jnp.exp2(s - _tile_lanes(m_new, reps))
    l_new = alpha * l_prev + jnp.sum(p, axis=1, keepdims=True)
    return p, alpha, m_new, l_new


def _moba_top_blocks(gate, n_past):
    lane = lax.broadcasted_iota(I32, gate.shape, 1)
    g = jnp.where(lane < n_past, gate, -jnp.inf)
    ind = jnp.zeros(gate.shape, F32)
    for _ in range(MOBA_TOPK):
        mx = jnp.max(g, axis=1, keepdims=True)
        am = jnp.min(jnp.where(g == mx, lane, 2 * LANES), axis=1, keepdims=True)
        hit = jnp.where(mx > -jnp.inf, jnp.where(lane == am, 1.0, 0.0), 0.0)
        ind = ind + hit
        g = jnp.where(lane == am, -jnp.inf, g)
    return ind


def _block_expand(j, kc):
    r_i = lax.broadcasted_iota(I32, (LANES, kc), 0)
    c_i = lax.broadcasted_iota(I32, (LANES, kc), 1)
    return jnp.where(r_i == (j * kc + c_i) // MOBA_BLOCK, 1.0, 0.0).astype(BF16)


def _flash_kernel(*refs, mode, tq, kc, nj, npairs):
    i = pl.program_id(0)
    j = pl.program_id(1)
    q_ref, k_ref, v_ref = refs[0], refs[1], refs[2]
    pos = 3
    if mode == "bias":
        b_ref = refs[pos]; pos += 1
    elif mode == "moba":
        km_ref = refs[pos]; pos += 1
    elif mode == "diff":
        lam_ref = refs[pos]; pos += 1
    o_ref = refs[pos]; pos += 1
    m_sc, l_sc, acc_sc = refs[pos], refs[pos + 1], refs[pos + 2]
    if mode == "moba":
        sel_sc = refs[pos + 3]
    jmax = ((i + 1) * tq - 1) // kc
    jdiag = (i * tq) // kc
    reps = kc // LANES
    nblk = kc // MOBA_BLOCK
    lane = lax.broadcasted_iota(I32, (tq, LANES), 1)
    lo = lane < 64

    def q_maps(p, scale):
        q2 = q_ref[:, p * LANES:(p + 1) * LANES]
        if scale != 1.0:
            q2 = q2 * scale
        return (jnp.where(lo, q2, 0.0).astype(BF16), jnp.where(lo, 0.0, q2).astype(BF16))

    @pl.when(j == 0)
    def _():
        m_sc[...] = jnp.full(m_sc.shape, NEG, F32)
        l_sc[...] = jnp.zeros(l_sc.shape, F32)
        acc_sc[...] = jnp.zeros(acc_sc.shape, F32)
        if mode == "moba":
            n_past = (i * tq + lax.broadcasted_iota(I32, (tq, 1), 0)) // MOBA_BLOCK
            for p in range(npairs):
                km2 = km_ref[:, p * LANES:(p + 1) * LANES].astype(BF16)
                for e, qm in enumerate(q_maps(p, 1.0)):
                    sel_sc[2 * p + e] = _moba_top_blocks(_dot_nt(qm, km2), n_past)

    def step(diag):
        if mode == "moba":
            row_t = i * tq + lax.broadcasted_iota(I32, (tq, MOBA_BLOCK), 0)
            lane_b = lax.broadcasted_iota(I32, (tq, MOBA_BLOCK), 1)
        if mode == "bias":
            bias = b_ref[0].astype(F32)
        elif diag and mode == "diff":
            causal = (j * kc + lax.broadcasted_iota(I32, (tq, kc), 1)
                      <= i * tq + lax.broadcasted_iota(I32, (tq, kc), 0))
        for p in range(npairs):
            k2 = k_ref[:, p * LANES:(p + 1) * LANES]
            v2 = v_ref[:, p * LANES:(p + 1) * LANES]
            pv, al = [], []
            for e, qm in enumerate(q_maps(p, Q_SCALE)):
                h = 2 * p + e
                s = _dot_nt(qm, k2)
                if mode == "bias":
                    s = s + bias
                elif mode == "moba":
                    sel = sel_sc[h]
                    parts = []
                    for c in range(nblk):
                        blk = j * nblk + c
                        sb = s[:, c * MOBA_BLOCK:(c + 1) * MOBA_BLOCK]
                        picked = jnp.sum(jnp.where(lane == blk, sel, 0.0), axis=1, keepdims=True)
                        sc = sb + jnp.where(picked > 0.5, 0.0, NEG)
                        if diag:
                            own = jnp.where(blk * MOBA_BLOCK + lane_b <= row_t, sb, NEG)
                            sc = jnp.where(row_t // MOBA_BLOCK == blk, own, sc)
                        parts.append(sc)
                    s = jnp.concatenate(parts, axis=1) if nblk > 1 else parts[0]
                elif diag:
                    s = jnp.where(causal, s, NEG)
                pr, alpha, m_new, l_new = _online_softmax(s, m_sc[h], l_sc[h], reps)
                m_sc[h] = m_new
                l_sc[h] = l_new
                pv.append(_dot(pr.astype(BF16), v2))
                al.append(alpha)
            if mode == "diff":
                acc_sc[2 * p] = acc_sc[2 * p] * al[0] + pv[0]
                acc_sc[2 * p + 1] = acc_sc[2 * p + 1] * al[1] + pv[1]
            else:
                acc_sc[p] = acc_sc[p] * jnp.where(lo, al[0], al[1]) + jnp.where(lo, pv[0], pv[1])

    if mode == "bias":
        pl.when(j <= jmax)(lambda: step(False))
    else:
        pl.when(j < jdiag)(lambda: step(False))
        pl.when(jnp.logical_and(j >= jdiag, j <= jmax))(lambda: step(True))

    @pl.when(j == nj - 1)
    def _():
        for p in range(npairs):
            if mode == "diff":
                o = acc_sc[2 * p] / l_sc[2 * p] - lam_ref[0, 0] * (acc_sc[2 * p + 1] / l_sc[2 * p + 1])
            else:
                o = acc_sc[p] / jnp.where(lo, l_sc[2 * p], l_sc[2 * p + 1])
            o_ref[:, p * LANES:(p + 1) * LANES] = o


def _flash_tiles(mode, S):
    tq, kc = {"bias": (512, 1024), "moba": (512, 1024), "diff": (512, 1024)}[mode]
    return min(tq, S), min(kc, S)


def _flash(q, k_bf, v_bf, mode, bias=None, kmean=None, lam=None):
    S, W = q.shape
    tq, kc = _flash_tiles(mode, S)
    assert kc % MOBA_BLOCK == 0 and tq % MOBA_BLOCK == 0
    ni, nj = S // tq, S // kc
    npairs = W // LANES
    jm = lambda i, j: jnp.minimum(j, ((i + 1) * tq - 1) // kc)
    in_specs = [pl.BlockSpec((tq, W), lambda i, j: (i, 0)),
                pl.BlockSpec((kc, W), lambda i, j: (jm(i, j), 0)),
                pl.BlockSpec((kc, W), lambda i, j: (jm(i, j), 0))]
    args = [q, k_bf, v_bf]
    nmaps = 2 * npairs
    scratch = [pltpu.VMEM((nmaps, tq, LANES), F32), pltpu.VMEM((nmaps, tq, LANES), F32),
               pltpu.VMEM((nmaps if mode == "diff" else npairs, tq, LANES), F32)]
    if mode == "bias":
        in_specs.append(pl.BlockSpec((1, tq, kc), lambda i, j: (jm(i, j), i, 0)))
        args.append(bias)
    elif mode == "moba":
        in_specs.append(pl.BlockSpec(kmean.shape, lambda i, j: (0, 0)))
        args.append(kmean)
        scratch.append(pltpu.VMEM((nmaps, tq, LANES), F32))
    elif mode == "diff":
        in_specs.append(pl.BlockSpec(memory_space=pltpu.SMEM))
        args.append(lam.reshape(1, 1))
    return pl.pallas_call(
        functools.partial(_flash_kernel, mode=mode, tq=tq, kc=kc, nj=nj, npairs=npairs),
        grid=(ni, nj), in_specs=in_specs,
        out_specs=pl.BlockSpec((tq, W), lambda i, j: (i, 0)),
        out_shape=jax.ShapeDtypeStruct((S, W), F32), scratch_shapes=scratch,
        compiler_params=_cparams(("parallel", "arbitrary"),
                                 VMEM_LIMIT_LARGE if mode == "diff" else VMEM_LIMIT),
        name="flash_" + mode)(*args)


def _block_mean_kernel(k_ref, o_ref, *, nb, per):
    i = pl.program_id(0)

    @pl.when(i < nb // per)
    def _():
        x = k_ref[...]
        o_ref[...] = jnp.mean(x.reshape(per, MOBA_BLOCK, x.shape[1]), axis=1)

    @pl.when(i >= nb // per)
    def _():
        o_ref[...] = jnp.zeros(o_ref.shape, F32)


def _block_means(kb):
    S, W = kb.shape
    nb, per = S // MOBA_BLOCK, 8
    last = nb // per - 1
    return pl.pallas_call(
        functools.partial(_block_mean_kernel, nb=nb, per=per),
        grid=(LANES // per,),
        in_specs=[pl.BlockSpec((per * MOBA_BLOCK, W), lambda i: (jnp.minimum(i, last), 0))],
        out_specs=pl.BlockSpec((per, W), lambda i: (i, 0)),
        out_shape=jax.ShapeDtypeStruct((LANES, W), F32),
        compiler_params=_cparams(("arbitrary",)), name="block_means")(kb)


def _page_specs(shape, ppc, nj):
    zeros = (0,) * len(shape)

    def make(p):
        return pl.BlockSpec((1,) + tuple(shape),
                            lambda b, j, pt: (pt[b, jnp.minimum(j, nj - 1) * ppc + p],) + zeros)
    return [make(p) for p in range(ppc)]


def _pool_t(cache):
    NP = cache.shape[0]
    nd = cache.ndim
    return jnp.transpose(cache, (0,) + tuple(range(2, nd)) + (1,)).reshape(NP, -1, PAGE)


DEC_CHUNK_PAGES = 4


def _decode_ppc(n_pages, want):
    ppc = min(want, n_pages)
    assert ppc % DEC_CHUNK_PAGES == 0 and n_pages % ppc == 0
    return ppc


def _idx_scores_kernel(pt_ref, q_ref, w_ref, new_ref, *refs, ppc, nj, T):
    j = pl.program_id(1)
    pages, o_ref = refs[:ppc], refs[ppc]
    cpp = DEC_CHUNK_PAGES
    kc = cpp * PAGE
    q = q_ref[0].astype(BF16)
    w = w_ref[0]

    def scores(kt):
        d = jnp.maximum(_dot(q, kt), 0.0) * _tile_lanes(w, kt.shape[1] // LANES)
        return jnp.sum(d.reshape(T, N_IDX, kt.shape[1]), axis=1)

    @pl.when(j < nj)
    def _():
        for c in range(ppc // cpp):
            kt = jnp.concatenate([r[0] for r in pages[c * cpp:(c + 1) * cpp]], axis=1).astype(BF16)
            o_ref[c, 0] = scores(kt)

    @pl.when(j == nj)
    def _():
        sc = scores(new_ref[0].astype(BF16))
        o_ref[0, 0] = jnp.concatenate([sc, jnp.zeros((T, kc - PAGE), F32)], axis=1)
        for c in range(1, ppc // cpp):
            o_ref[c, 0] = jnp.zeros((T, kc), F32)


def _idx_scores(pt, q_rows, w_rows, new_page, pool, ppc):
    B = q_rows.shape[0]
    T = q_rows.shape[1] // N_IDX
    nj = pt.shape[1] // ppc
    cps = ppc // DEC_CHUNK_PAGES
    kc = DEC_CHUNK_PAGES * PAGE
    grid_spec = pltpu.PrefetchScalarGridSpec(
        num_scalar_prefetch=1, grid=(B, nj + 1),
        in_specs=[pl.BlockSpec((1, T * N_IDX, HEAD_DIM), lambda b, j, pt: (b, 0, 0)),
                  pl.BlockSpec((1, T * N_IDX, LANES), lambda b, j, pt: (b, 0, 0)),
                  pl.BlockSpec((1, HEAD_DIM, PAGE), lambda b, j, pt: (b, 0, 0))]
        + _page_specs(pool.shape[1:], ppc, nj),
        out_specs=pl.BlockSpec((cps, 1, T, kc), lambda b, j, pt: (j, b, 0, 0)))
    return pl.pallas_call(
        functools.partial(_idx_scores_kernel, ppc=ppc, nj=nj, T=T),
        grid_spec=grid_spec, out_shape=jax.ShapeDtypeStruct(((nj + 1) * cps, B, T, kc), F32),
        compiler_params=_cparams(("parallel", "arbitrary")), name="idx_scores",
    )(pt, q_rows, w_rows, new_page, *([pool] * ppc))


def _pool_block_mean_kernel(pt_ref, *refs, npg):
    j = pl.program_id(1)
    pages, o_ref = refs[:npg], refs[npg]
    ppb = MOBA_BLOCK // PAGE
    lane = lax.broadcasted_iota(I32, o_ref.shape[1:], 1)

    @pl.when(j == 0)
    def _():
        o_ref[...] = jnp.zeros(o_ref.shape, F32)

    acc = o_ref[0]
    for blk in range(npg // ppb):
        tot = pages[blk * ppb][0]
        for a in range(1, ppb):
            tot = tot + pages[blk * ppb + a][0]
        mean = jnp.sum(tot, axis=1, keepdims=True) * (1.0 / MOBA_BLOCK)
        acc = jnp.where(lane == j * (npg // ppb) + blk, mean, acc)
    o_ref[0] = acc


def _pool_block_means(pt, pool_t, bpc=8):
    B, n_pages = pt.shape
    W = pool_t.shape[1]
    ppb = MOBA_BLOCK // PAGE
    nb = n_pages // ppb
    assert nb <= LANES
    bpc = min(bpc, nb)
    npg = bpc * ppb
    nj = nb // bpc
    grid_spec = pltpu.PrefetchScalarGridSpec(
        num_scalar_prefetch=1, grid=(B, nj),
        in_specs=_page_specs((W, PAGE), npg, nj),
        out_specs=pl.BlockSpec((1, W, LANES), lambda b, j, pt: (b, 0, 0)))
    return pl.pallas_call(
        functools.partial(_pool_block_mean_kernel, npg=npg),
        grid_spec=grid_spec, out_shape=jax.ShapeDtypeStruct((B, W, LANES), F32),
        compiler_params=_cparams(("parallel", "arbitrary")), name="pool_block_means",
    )(pt, *([pool_t] * npg))


def _paged_attn_kernel(pt_ref, *refs, mode, ppc, nj, T, G, W, Wv, vt):
    j = pl.program_id(1)
    q_ref, knew_ref, vnew_ref = refs[0], refs[1], refs[2]
    pos = 3
    if mode == "bias":
        b_ref = refs[pos]; pos += 1
    elif mode == "moba":
        km_ref = refs[pos]; pos += 1
    kpages = refs[pos:pos + ppc]; pos += ppc
    vpages = refs[pos:pos + ppc]; pos += ppc
    o_ref = refs[pos]; pos += 1
    m_sc, l_sc, acc_sc = refs[pos], refs[pos + 1], refs[pos + 2]
    if mode == "moba":
        sel_sc = refs[pos + 3]
    R = T * G
    kc = ppc * PAGE
    row = lax.broadcasted_iota(I32, (R, W), 0)
    col = lax.broadcasted_iota(I32, (R, W), 1)
    q = q_ref[0]
    qrep = jnp.concatenate([jnp.broadcast_to(q[t:t + 1], (G, W)) for t in range(T)], axis=0)
    qraw = jnp.where(col // HEAD_DIM == row % G, qrep, 0.0)
    qbd = (qraw * Q_SCALE).astype(BF16)

    @pl.when(j == 0)
    def _():
        m_sc[...] = jnp.full(m_sc.shape, NEG, F32)
        l_sc[...] = jnp.zeros(l_sc.shape, F32)
        acc_sc[...] = jnp.zeros(acc_sc.shape, F32)
        if mode == "moba":
            n_full = (nj * kc) // MOBA_BLOCK
            gate = _dot(qraw.astype(BF16), km_ref[0].astype(BF16))
            sel_sc[...] = _moba_top_blocks(gate, n_full).astype(BF16)

    def update(s, v):
        pr, alpha, m_new, l_new = _online_softmax(s, m_sc[...], l_sc[...], s.shape[1] // LANES)
        m_sc[...] = m_new
        l_sc[...] = l_new
        pv = _dot_nt(pr.astype(BF16), v) if vt else _dot(pr.astype(BF16), v)
        acc_sc[...] = acc_sc[...] * _tile_lanes(alpha, Wv // LANES) + pv

    @pl.when(j < nj)
    def _():
        k = jnp.concatenate([r[0] for r in kpages], axis=1).astype(BF16)
        v = jnp.concatenate([r[0] for r in vpages], axis=1 if vt else 0).astype(BF16)
        s = _dot(qbd, k)
        if mode == "bias":
            b = jnp.concatenate([b_ref[c, 0] for c in range(ppc // DEC_CHUNK_PAGES)], axis=1).astype(F32)
            s = s + jnp.concatenate([jnp.broadcast_to(b[t:t + 1], (G, kc)) for t in range(T)], axis=0)
        elif mode == "moba":
            s = jnp.where(_dot(sel_sc[...], _block_expand(j, kc)) > 0.5, s, NEG)
        update(s, v)

    @pl.when(j == nj)
    def _():
        s = _dot(qbd, knew_ref[0].astype(BF16))
        if mode == "bias":
            b = b_ref[0, 0][:, :PAGE].astype(F32)
            s = s + jnp.concatenate([jnp.broadcast_to(b[t:t + 1], (G, PAGE)) for t in range(T)], axis=0)
        else:
            r_i = lax.broadcasted_iota(I32, (R, PAGE), 0)
            c_i = lax.broadcasted_iota(I32, (R, PAGE), 1)
            s = jnp.where(c_i <= r_i // G, s, NEG)
        update(s, vnew_ref[0].astype(BF16))
        o = acc_sc[...] / _tile_lanes(l_sc[...], Wv // LANES)
        rv = lax.broadcasted_iota(I32, (R, Wv), 0)
        cv = lax.broadcasted_iota(I32, (R, Wv), 1)
        o = jnp.where(cv // HEAD_DIM == rv % G, o, 0.0)
        o_ref[0] = jnp.sum(o.reshape(T, G, Wv), axis=1)


def _paged_diff_kernel(pt_ref, q_ref, knew_ref, vnew_ref, lam_ref, *refs, ppc, nj, T, H, W):
    j = pl.program_id(1)
    kpages, vpages = refs[:ppc], refs[ppc:2 * ppc]
    o_ref, m_sc, l_sc, acc_sc = refs[2 * ppc:2 * ppc + 4]
    G = 2 * H
    R = G * T
    RH = 2 * T
    row = lax.broadcasted_iota(I32, (R, W), 0)
    col = lax.broadcasted_iota(I32, (R, W), 1)
    qrep = jnp.concatenate([q_ref[0]] * G, axis=0)
    qbd = (jnp.where(col // HEAD_DIM == row // T, qrep, 0.0) * Q_SCALE).astype(BF16)

    @pl.when(j == 0)
    def _():
        m_sc[...] = jnp.full(m_sc.shape, NEG, F32)
        l_sc[...] = jnp.zeros(l_sc.shape, F32)
        acc_sc[...] = jnp.zeros(acc_sc.shape, F32)

    def update(s, vrefs):
        pr, alpha, m_new, l_new = _online_softmax(s, m_sc[...], l_sc[...], s.shape[1] // LANES)
        m_sc[...] = m_new
        l_sc[...] = l_new
        prb = pr.astype(BF16)
        for h in range(H):
            vh = jnp.concatenate([r[0, pl.ds(h, PAGE, stride=H), :] for r in vrefs],
                                 axis=0).astype(BF16)
            rows = slice(h * RH, (h + 1) * RH)
            acc_sc[rows, :] = acc_sc[rows, :] * alpha[rows, :] + _dot(prb[rows, :], vh)

    @pl.when(j < nj)
    def _():
        k = jnp.concatenate([r[0] for r in kpages], axis=1).astype(BF16)
        update(_dot(qbd, k), vpages)

    @pl.when(j == nj)
    def _():
        s = _dot(qbd, knew_ref[0].astype(BF16))
        r_i = lax.broadcasted_iota(I32, (R, PAGE), 0)
        c_i = lax.broadcasted_iota(I32, (R, PAGE), 1)
        update(jnp.where(c_i <= r_i % T, s, NEG), [vnew_ref])
        o = acc_sc[...] / l_sc[...]
        outs = []
        for h in range(H):
            outs.append(o[h * RH:h * RH + T, :] - lam_ref[0, 0] * o[h * RH + T:(h + 1) * RH, :])
        o_ref[0] = jnp.concatenate(outs, axis=1)


def _paged_diff(pt, q3, knew, vnew, pool_k, pool_v, lam, ppc):
    B, T, W = q3.shape
    E = pool_v.shape[2]
    H = pool_v.shape[1] // PAGE
    assert E == LANES and W == 2 * H * HEAD_DIM and (2 * T) % 8 == 0
    nj = pt.shape[1] // ppc
    cmap3 = lambda b, j, pt: (b, 0, 0)
    R = 2 * H * T
    grid_spec = pltpu.PrefetchScalarGridSpec(
        num_scalar_prefetch=1, grid=(B, nj + 1),
        in_specs=[pl.BlockSpec((1, T, W), cmap3), pl.BlockSpec((1, W, PAGE), cmap3),
                  pl.BlockSpec((1, PAGE * H, E), cmap3), pl.BlockSpec(memory_space=pltpu.SMEM)]
        + _page_specs(pool_k.shape[1:], ppc, nj) + _page_specs(pool_v.shape[1:], ppc, nj),
        out_specs=pl.BlockSpec((1, T, H * E), cmap3),
        scratch_shapes=[pltpu.VMEM((R, LANES), F32), pltpu.VMEM((R, LANES), F32), pltpu.VMEM((R, E), F32)])
    return pl.pallas_call(
        functools.partial(_paged_diff_kernel, ppc=ppc, nj=nj, T=T, H=H, W=W),
        grid_spec=grid_spec, out_shape=jax.ShapeDtypeStruct((B, T, H * E), F32),
        compiler_params=_cparams(("parallel", "arbitrary")), name="paged_diff",
    )(pt, q3, knew, vnew, lam.reshape(1, 1), *([pool_k] * ppc), *([pool_v] * ppc))


def _paged_attn(pt, q3, knew, vnew, pool_k, pool_v, mode, vt, ppc, bias=None, kmean=None):
    B, T, W = q3.shape
    Wv = pool_v.shape[1] if vt else pool_v.shape[2]
    G = W // HEAD_DIM
    nj = pt.shape[1] // ppc
    cps = ppc // DEC_CHUNK_PAGES
    cmap = lambda b, j, pt: (b, 0, 0)
    in_specs = [pl.BlockSpec((1, T, W), cmap), pl.BlockSpec((1,) + knew.shape[1:], cmap),
                pl.BlockSpec((1,) + vnew.shape[1:], cmap)]
    args = [q3, knew, vnew]
    scratch = [pltpu.VMEM((T * G, LANES), F32), pltpu.VMEM((T * G, LANES), F32),
               pltpu.VMEM((T * G, Wv), F32)]
    if mode == "bias":
        assert bias.shape[0] == (nj + 1) * cps
        in_specs.append(pl.BlockSpec((cps, 1, T, bias.shape[3]), lambda b, j, pt: (j, b, 0, 0)))
        args.append(bias)
    elif mode == "moba":
        in_specs.append(pl.BlockSpec((1, W, LANES), cmap))
        args.append(kmean)
        scratch.append(pltpu.VMEM((T * G, LANES), BF16))
    in_specs += _page_specs(pool_k.shape[1:], ppc, nj) + _page_specs(pool_v.shape[1:], ppc, nj)
    args += [pool_k] * ppc + [pool_v] * ppc
    grid_spec = pltpu.PrefetchScalarGridSpec(
        num_scalar_prefetch=1, grid=(B, nj + 1), in_specs=in_specs,
        out_specs=pl.BlockSpec((1, T, Wv), cmap), scratch_shapes=scratch)
    return pl.pallas_call(
        functools.partial(_paged_attn_kernel, mode=mode, ppc=ppc, nj=nj, T=T, G=G, W=W, Wv=Wv, vt=vt),
        grid_spec=grid_spec, out_shape=jax.ShapeDtypeStruct((B, T, Wv), F32),
        compiler_params=_cparams(("parallel", "arbitrary")), name="paged_" + mode)(pt, *args)


def _lambda_kernel(a_ref, o_ref, *, lam_init):
    a = a_ref[...]
    d1 = jnp.sum(a[0:1] * a[1:2], axis=1, keepdims=True)
    d2 = jnp.sum(a[2:3] * a[3:4], axis=1, keepdims=True)
    o_ref[...] = jnp.exp(d1) - jnp.exp(d2) + lam_init


def _diff_lambda(lq1, lk1, lq2, lk2, lam_init):
    a = jnp.zeros((8, LANES), F32).at[:4, :HEAD_DIM].set(jnp.stack([lq1, lk1, lq2, lk2]))
    return pl.pallas_call(functools.partial(_lambda_kernel, lam_init=lam_init),
                          out_shape=jax.ShapeDtypeStruct((1, 1), F32), name="diff_lambda")(a)


def _pad_new_page(x3):
    B, T, W = x3.shape
    return jnp.pad(x3, ((0, 0), (0, PAGE - T), (0, 0)))


def _new_page_t(x3):
    B, T, W = x3.shape
    return jnp.pad(jnp.swapaxes(x3, 1, 2), ((0, 0), (0, 0), (0, PAGE - T)))


def kernel(x_prompt, x_sample, cache_a_k, cache_a_v, cache_idx_k, cache_b_k, cache_b_v, cache_c_k, cache_c_v, cache_mem_k, cache_mem_v, page_table, mem_prompt, g_mix, g_mem, g_final, w_in_even, w_out_even, w_in_odd, w_out_odd, lam_q1, lam_k1, lam_q2, lam_k2, g_subln, w_mq, w_mk, w_mv, w_mo):
    Bp, S, D = x_prompt.shape
    Bs, T, _ = x_sample.shape
    assert Bp == 1
    depth = g_mix.shape[0]
    n_pages = page_table.shape[1]
    past = n_pages * PAGE
    assert past % MOBA_BLOCK == 0 and S % MOBA_BLOCK == 0
    NP = cache_a_k.shape[1]
    HA = cache_a_k.shape[3]
    HB = cache_b_k.shape[3]
    HC = cache_c_k.shape[3]
    HM, HDM = cache_mem_k.shape[3], cache_mem_k.shape[4]
    NM = cache_mem_k.shape[2]
    WA, WB, WC, WM = HA * HEAD_DIM, HB * HEAD_DIM, HC * 2 * HEAD_DIM, HM * HDM
    WIQ = N_IDX * HEAD_DIM
    Ms = Bs * T

    cos_p, sin_p = _rope_tables(jnp.arange(S))
    cos_s, sin_s = _rope_tables(jnp.tile(past + jnp.arange(T), Bs))
    xp = x_prompt.reshape(S, D)
    xs = x_sample.reshape(Ms, D)
    mem2 = mem_prompt.reshape(NM, D)
    pt = page_table.astype(I32)

    n_tail = 4 * WA + 4 * WB + WIQ
    even_groups = []
    for gi in range(8):
        kind = "rope" if gi in (0, 1, 4, 5) else "plain"
        even_groups.append((gi * WA, WA, kind, 1.0, gi in (1, 2, 5, 6)))
    even_groups.append((8 * WA, WIQ, "rope", 1.0, False))
    even_groups.append((n_tail, LANES, "tail", float(WIQ) ** -0.5, True))
    odd_groups = [(0, WC, "rope", 1.0, False), (WC, WC, "rope", 1.0, True),
                  (2 * WC, WC, "plain", 1.0, True), (3 * WC, WC, "plain", 1.0, False)]
    mem_groups = [(0, WM, "plain", 1.0, False), (WM, WM, "plain", 1.0, False)]

    outs = {}
    ones = jnp.ones((D,), F32)
    for l in range(depth):
        i = l // 2
        if l % 2 == 0:
            w_in = w_in_even[i]
            w_in = jnp.pad(w_in, ((0, 0), (0, n_tail + LANES - w_in.shape[1]))).astype(BF16)
            w_out = w_out_even[i].astype(BF16)
            (qa, ka, ka_b, va, va_b, ga, qb, kb, kb_b, vb, vb_b, gb, qi, tail, tail_b) = _rms_proj(
                xp, g_mix[l], w_in, even_groups, cos_p, sin_p)
            kk = min(TOPK_TOK, S // 4)
            bias = _dsa_select(qi, tail, tail_b, kk, _flash_tiles("bias", S)[1])
            oa = _flash(qa, ka_b, va_b, "bias", bias=bias)
            ob = _flash(qb, kb_b, vb_b, "moba", kmean=_block_means(kb))
            xp = _gated_out(xp, w_out, [(oa, ga), (ob, gb)])
            outs.setdefault("pa_k", []).append(ka.reshape(1, S, HA, HEAD_DIM))
            outs.setdefault("pa_v", []).append(va.reshape(1, S, HA, HEAD_DIM))
            outs.setdefault("pidx_k", []).append(tail[:, :HEAD_DIM].reshape(1, S, HEAD_DIM))
            outs.setdefault("pb_k", []).append(kb.reshape(1, S, HB, HEAD_DIM))
            outs.setdefault("pb_v", []).append(vb.reshape(1, S, HB, HEAD_DIM))
            (qa, ka, _, va, _, ga, qb, kb, _, vb, _, gb, qi, tail, _) = _rms_proj(
                xs, g_mix[l], w_in, even_groups, cos_s, sin_s)
            q_rows = qi.reshape(Bs, T * N_IDX, HEAD_DIM)
            w_rows = jnp.broadcast_to(tail[:, HEAD_DIM:HEAD_DIM + N_IDX].reshape(Bs, T * N_IDX, 1),
                                      (Bs, T * N_IDX, LANES))
            tail3 = tail.reshape(Bs, T, LANES)
            ki_new = _new_page_t(tail3[:, :, :HEAD_DIM])
            ppc = _decode_ppc(n_pages, 16)
            sc = _idx_scores(pt, q_rows, w_rows, ki_new, _pool_t(cache_idx_k[i]), ppc)
            ncs, _, _, kcs = sc.shape
            lim = (past + jnp.tile(jnp.arange(T, dtype=I32), Bs)).reshape(Ms, 1)
            bias_s = _select_rows(sc.reshape(ncs, Ms, kcs), lim, min(TOPK_TOK, (past + T) // 4))
            oa = _paged_attn(pt, qa.reshape(Bs, T, WA), _new_page_t(ka.reshape(Bs, T, WA)),
                             _new_page_t(va.reshape(Bs, T, WA)),
                             _pool_t(cache_a_k[i]), _pool_t(cache_a_v[i]),
                             "bias", True, ppc, bias=bias_s.reshape(ncs, Bs, T, kcs))
            pool_bk = _pool_t(cache_b_k[i])
            ob = _paged_attn(pt, qb.reshape(Bs, T, WB), _new_page_t(kb.reshape(Bs, T, WB)),
                             _new_page_t(vb.reshape(Bs, T, WB)),
                             pool_bk, _pool_t(cache_b_v[i]),
                             "moba", True, ppc, kmean=_pool_block_means(pt, pool_bk, 16))
            xs = _gated_out(xs, w_out, [(oa.reshape(Ms, WA), ga), (ob.reshape(Ms, WB), gb)])
            outs.setdefault("sa_k", []).append(ka.reshape(Bs, T, HA, HEAD_DIM))
            outs.setdefault("sa_v", []).append(va.reshape(Bs, T, HA, HEAD_DIM))
            outs.setdefault("sidx_k", []).append(tail[:, :HEAD_DIM].reshape(Bs, T, HEAD_DIM))
            outs.setdefault("sb_k", []).append(kb.reshape(Bs, T, HB, HEAD_DIM))
            outs.setdefault("sb_v", []).append(vb.reshape(Bs, T, HB, HEAD_DIM))
        else:
            lam_init = 0.8 - 0.6 * math.exp(-0.3 * l)
            lam = _diff_lambda(lam_q1[i], lam_k1[i], lam_q2[i], lam_k2[i], lam_init)
            w_in = w_in_odd[i].astype(BF16)
            w_out = w_out_odd[i].astype(BF16)
            q, k, k_b, v, v_b, g = _rms_proj(xp, g_mix[l], w_in, odd_groups, cos_p, sin_p)
            o = _flash(q, k_b, v_b, "diff", lam=lam)
            xp = _gated_out(xp, w_out, [(o, g)], gsub=g_subln[i], post_scale=1.0 - lam_init)
            outs.setdefault("pc_k", []).append(k.reshape(1, S, HC, 2, HEAD_DIM))
            outs.setdefault("pc_v", []).append(v.reshape(1, S, HC, 2 * HEAD_DIM))
            q, k, _, v, _, g = _rms_proj(xs, g_mix[l], w_in, odd_groups, cos_s, sin_s)
            o = _paged_diff(pt, q.reshape(Bs, T, WC), _new_page_t(k.reshape(Bs, T, WC)),
                            _pad_new_page(v.reshape(Bs, T, WC)).reshape(Bs, PAGE * HC, 2 * HEAD_DIM),
                            _pool_t(cache_c_k[i]), cache_c_v[i].reshape(NP, PAGE * HC, 2 * HEAD_DIM), lam,
                            _decode_ppc(n_pages, 8))
            xs = _gated_out(xs, w_out, [(o.reshape(Ms, WC), g)], gsub=g_subln[i], post_scale=1.0 - lam_init)
            outs.setdefault("sc_k", []).append(k.reshape(Bs, T, HC, 2, HEAD_DIM))
            outs.setdefault("sc_v", []).append(v.reshape(Bs, T, HC, 2 * HEAD_DIM))
        w_kv = jnp.concatenate([w_mk[l], w_mv[l]], axis=1).astype(BF16)
        mk, mv = _rms_proj(mem2, ones, w_kv, mem_groups, norm=False)
        w_q = w_mq[l].astype(BF16)
        w_o = w_mo[l].astype(BF16)
        last = l == depth - 1
        q, g = _rms_proj(xp, g_mem[l], w_q, mem_groups)
        o = _mem_attn(q.reshape(1, S, WM), mk.reshape(1, NM, WM), mv.reshape(1, NM, WM), HM, HDM)
        res = _gated_out(xp, w_o, [(o.reshape(S, WM), g)], gfin=g_final if last else None)
        xp, yp = res if last else (res, None)
        q, g = _rms_proj(xs, g_mem[l], w_q, mem_groups)
        o = _mem_attn(q.reshape(Bs, T, WM), cache_mem_k[l].reshape(Bs, NM, WM),
                      cache_mem_v[l].reshape(Bs, NM, WM), HM, HDM)
        res = _gated_out(xs, w_o, [(o.reshape(Ms, WM), g)], gfin=g_final if last else None)
        xs, ys = res if last else (res, None)
        outs.setdefault("pm_k", []).append(mk.reshape(1, NM, HM, HDM))
        outs.setdefault("pm_v", []).append(mv.reshape(1, NM, HM, HDM))

    st = lambda name: jnp.stack(outs[name])
    return (yp.reshape(1, S, D), ys.reshape(Bs, T, D),
            st("pa_k"), st("pa_v"), st("pidx_k"), st("pb_k"), st("pb_v"), st("pc_k"), st("pc_v"),
            st("pm_k"), st("pm_v"),
            st("sa_k"), st("sa_v"), st("sidx_k"), st("sb_k"), st("sb_v"), st("sc_k"), st("sc_v"))
```
